```python
import jax, jax.numpy as jnp
from jax import lax
import numpy as np

D_MODEL = 4096
BATCH = 2
SEQ = 8192
DEPTH = 2

CTX_LEN = 256
GRID_W = 64

N_HEADS = 16
N_KV_HEADS = 4
HEAD_DIM = 128
ATTN_W = N_HEADS * HEAD_DIM
KV_W = N_KV_HEADS * HEAD_DIM
WINDOW = 128
BLOCK = 128
ROPE_BASE = 10000.0

FOURIER_GROUPS = 4
FOURIER_GROUP_W = 256
FOURIER_W = FOURIER_GROUPS * FOURIER_GROUP_W

CONV_W = 1024
CONV_K = 31

N_BRANCH = 3
EPS = 1e-6
NEG_INF = -1e30

Q_OFF = 0
K_OFF = Q_OFF + ATTN_W
V_OFF = K_OFF + KV_W
AG_OFF = V_OFF + KV_W
F_OFF = AG_OFF + ATTN_W
FG_OFF = F_OFF + FOURIER_W
CA_OFF = FG_OFF + FOURIER_W
CB_OFF = CA_OFF + CONV_W
CG_OFF = CB_OFF + CONV_W
MG_OFF = CG_OFF + CONV_W
IN_W = MG_OFF + N_BRANCH * D_MODEL

kernel_name = 'hybrid_fnet_swa_conformer_dit_block'


def rms_norm(x, g):
    xf = x.astype(jnp.float32)
    y = xf * lax.rsqrt(jnp.mean(jnp.square(xf), axis=-1, keepdims=True) + EPS)
    return (y * g.astype(jnp.float32)).astype(x.dtype)


def layer_norm(x, g, b):
    xf = x.astype(jnp.float32)
    mu = jnp.mean(xf, axis=-1, keepdims=True)
    var = jnp.mean(jnp.square(xf - mu), axis=-1, keepdims=True)
    y = (xf - mu) * lax.rsqrt(var + EPS)
    return (y * g.astype(jnp.float32) + b.astype(jnp.float32)).astype(x.dtype)


def adaln_modulate(x, g, shift, scale):
    return rms_norm(x, g) * (1 + scale) + shift


def axial_rope_tables(n_tok):
    rows = n_tok // GRID_W
    row = jnp.broadcast_to(jnp.arange(rows, dtype=jnp.float32)[:, None], (rows, GRID_W)).reshape(-1)
    col = jnp.broadcast_to(jnp.arange(GRID_W, dtype=jnp.float32)[None, :], (rows, GRID_W)).reshape(-1)
    axis_dim = HEAD_DIM // 2
    inv_freq = ROPE_BASE ** (-jnp.arange(0, axis_dim, 2, dtype=jnp.float32) / axis_dim)
    ang_r = row[:, None] * inv_freq[None, :]
    ang_c = col[:, None] * inv_freq[None, :]
    ang = jnp.concatenate([ang_r, ang_r, ang_c, ang_c], axis=-1)
    return jnp.cos(ang), jnp.sin(ang)


def _rotate_half(u):
    u1, u2 = jnp.split(u, 2, axis=-1)
    return jnp.concatenate([-u2, u1], axis=-1)


def apply_axial_rope(x, cos, sin):
    xf = x.astype(jnp.float32)
    xr, xc = jnp.split(xf, 2, axis=-1)
    rot = jnp.concatenate([_rotate_half(xr), _rotate_half(xc)], axis=-1)
    return (xf * cos[:, None, :] + rot * sin[:, None, :]).astype(x.dtype)


def kv_heads(pkv, k_g):
    B, L = pkv.shape[:2]
    k = rms_norm(pkv[..., :KV_W].reshape(B, L, N_KV_HEADS, HEAD_DIM), k_g)
    v = pkv[..., KV_W:].reshape(B, L, N_KV_HEADS, HEAD_DIM)
    return k, v


def attn_heads(p, q_g, k_g):
    B, L = p.shape[:2]
    q = rms_norm(p[..., Q_OFF:K_OFF].reshape(B, L, N_HEADS, HEAD_DIM), q_g)
    k, v = kv_heads(p[..., K_OFF:AG_OFF], k_g)
    return q, k, v


def context_attention(q, k, v, sink):
    B, C = q.shape[:2]
    G = N_HEADS // N_KV_HEADS
    qg = q.reshape(B, C, N_KV_HEADS, G, HEAD_DIM)
    s = jnp.einsum('bqhgd,bkhd->bhgqk', qg, k, preferred_element_type=jnp.float32) * (HEAD_DIM ** -0.5)
    sink_col = jnp.broadcast_to(sink.astype(jnp.float32).reshape(1, N_KV_HEADS, G, 1, 1), s.shape[:-1] + (1,))
    pr = jax.nn.softmax(jnp.concatenate([s, sink_col], axis=-1), axis=-1)[..., :-1]
    o = jnp.einsum('bhgqk,bkhd->bqhgd', pr.astype(v.dtype), v)
    return o.reshape(B, C, ATTN_W)


def windowed_attention_with_context(q, k, v, k_ctx, v_ctx, sink):
    B, L = q.shape[:2]
    nb = L // BLOCK
    G = N_HEADS // N_KV_HEADS
    scale = HEAD_DIM ** -0.5
    qb = q.reshape(B, nb, BLOCK, N_KV_HEADS, G, HEAD_DIM)
    pad = ((0, 0), (BLOCK, BLOCK), (0, 0), (0, 0))
    kb = jnp.pad(k, pad).reshape(B, nb + 2, BLOCK, N_KV_HEADS, HEAD_DIM)
    vb = jnp.pad(v, pad).reshape(B, nb + 2, BLOCK, N_KV_HEADS, HEAD_DIM)
    k_band = jnp.concatenate([kb[:, :-2], kb[:, 1:-1], kb[:, 2:]], axis=2)
    v_band = jnp.concatenate([vb[:, :-2], vb[:, 1:-1], vb[:, 2:]], axis=2)
    s_loc = jnp.einsum('bnqhgd,bnkhd->bnhgqk', qb, k_band, preferred_element_type=jnp.float32) * scale
    s_ctx = jnp.einsum('bnqhgd,bchd->bnhgqc', qb, k_ctx, preferred_element_type=jnp.float32) * scale
    qi = jnp.arange(BLOCK)[:, None]
    kj = jnp.arange(3 * BLOCK)[None, :]
    rel = kj - qi
    band = (rel >= BLOCK - WINDOW) & (rel <= BLOCK + WINDOW)
    kpos = jnp.arange(nb)[:, None] * BLOCK - BLOCK + jnp.arange(3 * BLOCK)[None, :]
    in_range = (kpos >= 0) & (kpos < L)
    mask = band[None, :, :] & in_range[:, None, :]
    s_loc = jnp.where(mask[None, :, None, None, :, :], s_loc, NEG_INF)
    sink_b = sink.astype(jnp.float32).reshape(1, 1, N_KV_HEADS, G, 1, 1)
    m = jnp.maximum(jnp.maximum(jnp.max(s_loc, axis=-1, keepdims=True),
                                jnp.max(s_ctx, axis=-1, keepdims=True)), sink_b)
    p_loc = jnp.exp(s_loc - m)
    p_ctx = jnp.exp(s_ctx - m)
    denom = jnp.sum(p_loc, axis=-1, keepdims=True) + jnp.sum(p_ctx, axis=-1, keepdims=True) + jnp.exp(sink_b - m)
    inv = 1.0 / denom
    o = (jnp.einsum('bnhgqk,bnkhd->bnqhgd', (p_loc * inv).astype(v.dtype), v_band)
         + jnp.einsum('bnhgqc,bchd->bnqhgd', (p_ctx * inv).astype(v.dtype), v_ctx))
    return o.reshape(B, L, ATTN_W)


def fourier_mix(u, w_mix):
    B, L, _ = u.shape
    ug = u.astype(jnp.float32).reshape(B, L, FOURIER_GROUPS, FOURIER_GROUP_W)
    f = jnp.fft.fft2(ug, axes=(1, 3), norm='ortho').real
    y = jnp.einsum('blgc,gcd->blgd', f, w_mix.astype(jnp.float32))
    return y.reshape(B, L, FOURIER_W).astype(u.dtype)


def conformer_conv(a, b, dw_w, dw_b, ln_g, ln_b, w_pw):
    u = a * jax.nn.sigmoid(b)
    y = lax.conv_general_dilated(u, dw_w[:, None, :], window_strides=(1,),
                                 padding=[(CONV_K // 2, CONV_K // 2)],
                                 dimension_numbers=('NWC', 'WIO', 'NWC'),
                                 feature_group_count=CONV_W) + dw_b
    y = jax.nn.silu(layer_norm(y, ln_g, ln_b))
    return y @ w_pw


def merge_branches(p, attn_o, w_attn_up, w_fmix, w_four_up, dw_w, dw_b, cln_g, cln_b, w_pw, w_conv_up, w_out):
    B, L = p.shape[:2]
    y_attn = (attn_o * jax.nn.silu(p[..., AG_OFF:F_OFF])) @ w_attn_up
    y_four = (fourier_mix(p[..., F_OFF:FG_OFF], w_fmix) * jax.nn.silu(p[..., FG_OFF:CA_OFF])) @ w_four_up
    y_conv = (conformer_conv(p[..., CA_OFF:CB_OFF], p[..., CB_OFF:CG_OFF], dw_w, dw_b, cln_g, cln_b, w_pw)
              * jax.nn.silu(p[..., CG_OFF:MG_OFF])) @ w_conv_up
    g = jax.nn.sigmoid(p[..., MG_OFF:]).reshape(B, L, N_BRANCH, D_MODEL)
    merged = g[..., 0, :] * y_attn + g[..., 1, :] * y_four + g[..., 2, :] * y_conv
    return merged @ w_out


def setup_inputs(seed: int = 0) -> dict:
    key = jax.random.key(seed)
    ks = jax.random.split(key, 24)

    def nrm(k, shape, s):
        return jax.random.normal(k, shape, jnp.float32) * s

    return {
        'x': nrm(ks[0], (BATCH, SEQ, D_MODEL), 1.0),
        'c': nrm(ks[1], (BATCH, D_MODEL), 1.0),
        'ctx': nrm(ks[2], (BATCH, CTX_LEN, D_MODEL), 1.0),
        'c_ctx': nrm(ks[3], (D_MODEL,), 1.0),
        'norm_g': 1.0 + nrm(ks[4], (DEPTH, D_MODEL), 0.02),
        'w_mod': nrm(ks[5], (DEPTH, D_MODEL, 3 * D_MODEL), 0.5 * D_MODEL ** -0.5),
        'b_mod': nrm(ks[6], (DEPTH, 3 * D_MODEL), 0.02),
        'w_in': nrm(ks[7], (DEPTH, D_MODEL, IN_W), D_MODEL ** -0.5),
        'q_norm_g': 1.0 + nrm(ks[8], (DEPTH, HEAD_DIM), 0.02),
        'k_norm_g': 1.0 + nrm(ks[9], (DEPTH, HEAD_DIM), 0.02),
        'attn_sink': nrm(ks[10], (DEPTH, N_HEADS), 0.5),
        'w_attn_up': nrm(ks[11], (DEPTH, ATTN_W, D_MODEL), ATTN_W ** -0.5),
        'w_fourier_mix': nrm(ks[12], (DEPTH, FOURIER_GROUPS, FOURIER_GROUP_W, FOURIER_GROUP_W), FOURIER_GROUP_W ** -0.5),
        'w_fourier_up': nrm(ks[13], (DEPTH, FOURIER_W, D_MODEL), FOURIER_W ** -0.5),
        'conv_dw_w': nrm(ks[14], (DEPTH, CONV_K, CONV_W), CONV_K ** -0.5),
        'conv_dw_b': nrm(ks[15], (DEPTH, CONV_W), 0.02),
        'conv_ln_g': 1.0 + nrm(ks[16], (DEPTH, CONV_W), 0.02),
        'conv_ln_b': nrm(ks[17], (DEPTH, CONV_W), 0.02),
        'w_conv_pw': nrm(ks[18], (DEPTH, CONV_W, CONV_W), CONV_W ** -0.5),
        'w_conv_up': nrm(ks[19], (DEPTH, CONV_W, D_MODEL), CONV_W ** -0.5),
        'w_out': nrm(ks[20], (DEPTH, D_MODEL, D_MODEL), D_MODEL ** -0.5),
    }


def reference(x, c, ctx, c_ctx, norm_g, w_mod, b_mod, w_in, q_norm_g, k_norm_g, attn_sink,
              w_attn_up, w_fourier_mix, w_fourier_up, conv_dw_w, conv_dw_b, conv_ln_g, conv_ln_b,
              w_conv_pw, w_conv_up, w_out):
    n_lat = x.shape[1]
    cos, sin = axial_rope_tables(n_lat)
    for l in range(DEPTH):
        mod_lat = jax.nn.silu(c) @ w_mod[l] + b_mod[l]
        sh, sc, gt = jnp.split(mod_lat[:, None, :], 3, axis=-1)
        mod_ctx = jax.nn.silu(c_ctx) @ w_mod[l] + b_mod[l]
        sh_c, sc_c, gt_c = jnp.split(mod_ctx, 3)
        branch_w = (w_attn_up[l], w_fourier_mix[l], w_fourier_up[l], conv_dw_w[l], conv_dw_b[l],
                    conv_ln_g[l], conv_ln_b[l], w_conv_pw[l], w_conv_up[l], w_out[l])

        h_ctx = adaln_modulate(ctx, norm_g[l], sh_c, sc_c)
        if l < DEPTH - 1:
            p_ctx = h_ctx @ w_in[l]
            q_c, k_c, v_c = attn_heads(p_ctx, q_norm_g[l], k_norm_g[l])
            o_c = context_attention(q_c, k_c, v_c, attn_sink[l])
            ctx_next = ctx + gt_c * merge_branches(p_ctx, o_c, *branch_w)
        else:
            k_c, v_c = kv_heads(h_ctx @ w_in[l][:, K_OFF:AG_OFF], k_norm_g[l])
            ctx_next = ctx

        h = adaln_modulate(x, norm_g[l], sh, sc)
        p = h @ w_in[l]
        q, k, v = attn_heads(p, q_norm_g[l], k_norm_g[l])
        q = apply_axial_rope(q, cos, sin)
        k = apply_axial_rope(k, cos, sin)
        o = windowed_attention_with_context(q, k, v, k_c, v_c, attn_sink[l])
        x = x + gt * merge_branches(p, o, *branch_w)
        ctx = ctx_next
    return x
```

```python
import functools

import numpy as np
import jax
import jax.numpy as jnp
from jax import lax
from jax.experimental import pallas as pl
from jax.experimental.pallas import tpu as pltpu

D_MODEL = 4096
BATCH = 2
SEQ = 8192
DEPTH = 2
CTX_LEN = 256
GRID_W = 64

N_HEADS = 16
N_KV_HEADS = 4
HEAD_DIM = 128
GROUP = N_HEADS // N_KV_HEADS
ATTN_W = N_HEADS * HEAD_DIM
KV_W = N_KV_HEADS * HEAD_DIM
WINDOW = 128
BLOCK = 128
ROPE_BASE = 10000.0

FOURIER_GROUPS = 4
FOURIER_GROUP_W = 256
FOURIER_W = FOURIER_GROUPS * FOURIER_GROUP_W

CONV_W = 1024
CONV_K = 31
CONV_HALO = 16

N_BRANCH = 3
EPS = 1e-6
NEG_INF = -1e30

Q_OFF = 0
K_OFF = Q_OFF + ATTN_W
AG_OFF = K_OFF + 2 * KV_W
F_OFF = AG_OFF + ATTN_W
FG_OFF = F_OFF + FOURIER_W
CA_OFF = FG_OFF + FOURIER_W
CB_OFF = CA_OFF + CONV_W
CG_OFF = CB_OFF + CONV_W
MG_OFF = CG_OFF + CONV_W
IN_W = MG_OFF + N_BRANCH * D_MODEL

FFT_N1 = 128
FFT_N2 = 64
FFT_KB = 16

V7X_VMEM_LIMIT = 56 * 1024 * 1024
LANES = 128

BF16 = jnp.bfloat16
F32 = jnp.float32


def _cparams(n_axes):
    return pltpu.CompilerParams(dimension_semantics=("arbitrary",) * n_axes,
                                vmem_limit_bytes=V7X_VMEM_LIMIT)


def _silu(v):
    return v * jax.nn.sigmoid(v)


def _act(v, act):
    if act == "silu":
        return _silu(v)
    if act == "sigmoid":
        return jax.nn.sigmoid(v)
    return v


def _dot(a, b):
    return jnp.dot(a, b, preferred_element_type=F32)


def _dot_nt(a, b):
    return lax.dot_general(a, b, (((1,), (1,)), ((), ())), preferred_element_type=F32)


def _mod_kernel(c_ref, w_ref, b_ref, o_ref):
    s = _silu(c_ref[...]).astype(BF16)
    o_ref[...] = _dot(s, w_ref[...].astype(BF16)) + b_ref[...]


def _modulation(cc, w_mod, b_mod):
    tn = 512
    return pl.pallas_call(
        _mod_kernel,
        grid=(DEPTH, 3 * D_MODEL // tn),
        in_specs=[pl.BlockSpec((8, D_MODEL), lambda l, j: (0, 0)),
                  pl.BlockSpec((None, D_MODEL, tn), lambda l, j: (l, 0, j)),
                  pl.BlockSpec((None, 1, tn), lambda l, j: (l, 0, j))],
        out_specs=pl.BlockSpec((None, 8, tn), lambda l, j: (l, 0, j)),
        out_shape=jax.ShapeDtypeStruct((DEPTH, 8, 3 * D_MODEL), F32),
        compiler_params=_cparams(2),
        name="modulation",
    )(cc, w_mod, b_mod.reshape(DEPTH, 1, 3 * D_MODEL))


def _adaln_kernel(x_ref, g_ref, sh_ref, sc_ref, o_ref):
    x = x_ref[...]
    ms = jnp.mean(x * x, axis=-1, keepdims=True)
    y = x * lax.rsqrt(ms + EPS) * g_ref[...]
    o_ref[...] = (y * (1.0 + sc_ref[...]) + sh_ref[...]).astype(o_ref.dtype)


def _adaln(x2d, norm_g, mods, layer, row_of_tile, tm):
    m = x2d.shape[0]
    return pl.pallas_call(
        _adaln_kernel,
        grid=(m // tm,),
        in_specs=[pl.BlockSpec((tm, D_MODEL), lambda i: (i, 0)),
                  pl.BlockSpec((None, 1, D_MODEL), lambda i: (layer, 0, 0)),
                  pl.BlockSpec((None, None, 1, D_MODEL), lambda i: (layer, row_of_tile(i), 0, 0)),
                  pl.BlockSpec((None, None, 1, D_MODEL), lambda i: (layer, row_of_tile(i), 0, 1))],
        out_specs=pl.BlockSpec((tm, D_MODEL), lambda i: (i, 0)),
        out_shape=jax.ShapeDtypeStruct((m, D_MODEL), BF16),
        compiler_params=_cparams(1),
        name="adaln",
    )(x2d, norm_g.reshape(DEPTH, 1, D_MODEL), mods, mods)


def _proj_kernel(a_ref, w_ref, o_ref, *, act):
    o_ref[...] = _act(_dot(a_ref[...], w_ref[...]), act).astype(o_ref.dtype)


def _proj(h, w_in, layer, col_off, n_cols, act, tm):
    m, k = h.shape
    tn = 1024
    return pl.pallas_call(
        functools.partial(_proj_kernel, act=act),
        grid=(n_cols // tn, m // tm),
        in_specs=[pl.BlockSpec((tm, k), lambda j, i: (i, 0)),
                  pl.BlockSpec((None, k, tn), lambda j, i: (layer, 0, col_off // tn + j))],
        out_specs=pl.BlockSpec((tm, tn), lambda j, i: (i, j)),
        out_shape=jax.ShapeDtypeStruct((m, n_cols), BF16),
        compiler_params=_cparams(2),
        name="in_proj",
    )(h, w_in)


def _rms_head(x, g):
    ms = jnp.mean(x * x, axis=-1, keepdims=True)
    return x * lax.rsqrt(ms + EPS) * g


def _rope(y, cos, sin_signed):
    lane = lax.broadcasted_iota(jnp.int32, y.shape, 1)
    partner = jnp.where((lane & 32) == 0, pltpu.roll(y, 96, 1), pltpu.roll(y, 32, 1))
    return y * cos + partner * sin_signed


def _rope_tables():
    rows = SEQ // GRID_W
    row = np.repeat(np.arange(rows, dtype=np.float64), GRID_W)
    col = np.tile(np.arange(GRID_W, dtype=np.float64), rows)
    axis_dim = HEAD_DIM // 2
    inv_freq = ROPE_BASE ** (-np.arange(0, axis_dim, 2, dtype=np.float64) / axis_dim)
    ang_r = row[:, None] * inv_freq[None, :]
    ang_c = col[:, None] * inv_freq[None, :]
    ang = np.concatenate([ang_r, ang_r, ang_c, ang_c], axis=-1)
    sign = np.where((np.arange(HEAD_DIM) & 32) == 0, -1.0, 1.0)
    return (jnp.asarray(np.cos(ang), F32), jnp.asarray(np.sin(ang) * sign[None, :], F32))


ATTN_TQ = 512
ATTN_QB = ATTN_TQ // BLOCK
ATTN_BAND = 3 * BLOCK
ATTN_SCALE = HEAD_DIM ** -0.5


def _lat_attn_kernel(sink_ref, q_ref, kvc_ref, kvp_ref, kvn_ref, ag_ref, kvx_ref,
                     cosc_ref, sinc_ref, cosp_ref, sinp_ref, cosn_ref, sinn_ref, qg_ref, kg_ref,
                     o_ref, qbuf, kbuf, vbuf, kxbuf):
    t = pl.program_id(1)
    qg = qg_ref[...]
    kg = kg_ref[...]
    cosc, sinc = cosc_ref[...], sinc_ref[...]
    cosp, sinp = cosp_ref[...], sinp_ref[...]
    cosn, sinn = cosn_ref[...], sinn_ref[...]
    kv_rows = ATTN_TQ + 2 * BLOCK

    for h in range(N_KV_HEADS):
        ks = slice(h * HEAD_DIM, (h + 1) * HEAD_DIM)
        vs = slice(KV_W + h * HEAD_DIM, KV_W + (h + 1) * HEAD_DIM)
        kbuf[h, 0:BLOCK, :] = _rope(_rms_head(kvp_ref[:, ks].astype(F32), kg), cosp, sinp).astype(BF16)
        kbuf[h, BLOCK:BLOCK + ATTN_TQ, :] = _rope(_rms_head(kvc_ref[:, ks].astype(F32), kg), cosc, sinc).astype(BF16)
        kbuf[h, BLOCK + ATTN_TQ:kv_rows, :] = _rope(_rms_head(kvn_ref[:, ks].astype(F32), kg), cosn, sinn).astype(BF16)
        vbuf[h, 0:BLOCK, :] = kvp_ref[:, vs]
        vbuf[h, BLOCK:BLOCK + ATTN_TQ, :] = kvc_ref[:, vs]
        vbuf[h, BLOCK + ATTN_TQ:kv_rows, :] = kvn_ref[:, vs]
        kxbuf[h] = _rms_head(kvx_ref[:, ks].astype(F32), kg).astype(BF16)

    for hh in range(N_HEADS):
        h, j = divmod(hh, GROUP)
        qn = _rope(_rms_head(q_ref[:, hh * HEAD_DIM:(hh + 1) * HEAD_DIM].astype(F32), qg), cosc, sinc) * ATTN_SCALE
        for tb in range(ATTN_QB):
            qbuf[h, tb, j * BLOCK:(j + 1) * BLOCK, :] = qn[tb * BLOCK:(tb + 1) * BLOCK].astype(BF16)

    rows_g = GROUP * BLOCK
    row = lax.broadcasted_iota(jnp.int32, (rows_g, ATTN_BAND), 0)
    col = lax.broadcasted_iota(jnp.int32, (rows_g, ATTN_BAND), 1)
    qi = row & (BLOCK - 1)
    band = (col >= qi) & (col <= qi + 2 * WINDOW)
    rgrp = lax.broadcasted_iota(jnp.int32, (rows_g, 1), 0) >> 7

    for tb in range(ATTN_QB):
        kpos = (t * ATTN_QB + tb - 1) * BLOCK + col
        valid = band & (kpos >= 0) & (kpos < SEQ)
        for h in range(N_KV_HEADS):
            vs = slice(KV_W + h * HEAD_DIM, KV_W + (h + 1) * HEAD_DIM)
            qh = qbuf[h, tb]
            s_loc = jnp.where(valid, _dot_nt(qh, kbuf[h, tb * BLOCK:tb * BLOCK + ATTN_BAND, :]), NEG_INF)
            s_ctx = _dot_nt(qh, kxbuf[h])
            sink = jnp.full((rows_g, 1), sink_ref[h * GROUP], F32)
            for j in range(1, GROUP):
                sink = jnp.where(rgrp == j, sink_ref[h * GROUP + j], sink)
            m = jnp.maximum(jnp.maximum(jnp.max(s_loc, axis=-1, keepdims=True),
                                        jnp.max(s_ctx, axis=-1, keepdims=True)), sink)
            p_loc = jnp.exp(s_loc - m)
            p_ctx = jnp.exp(s_ctx - m)
            den = (jnp.sum(p_loc, axis=-1, keepdims=True) + jnp.sum(p_ctx, axis=-1, keepdims=True)
                   + jnp.exp(sink - m))
            o = (_dot(p_loc.astype(BF16), vbuf[h, tb * BLOCK:tb * BLOCK + ATTN_BAND, :])
                 + _dot(p_ctx.astype(BF16), kvx_ref[:, vs]))
            o = o * (1.0 / den)
            for j in range(GROUP):
                hh = h * GROUP + j
                rs = slice(tb * BLOCK, (tb + 1) * BLOCK)
                cs = slice(hh * HEAD_DIM, (hh + 1) * HEAD_DIM)
                o_ref[rs, cs] = (o[j * BLOCK:(j + 1) * BLOCK] * ag_ref[rs, cs].astype(F32)).astype(o_ref.dtype)


def _latent_attention(q, kv, ag, kv_ctx, sink, q_g, k_g, cos, sin_s):
    nq = SEQ // ATTN_TQ
    nb = SEQ // BLOCK

    def prev_blk(t):
        return jnp.maximum(t * ATTN_QB - 1, 0)

    def next_blk(t):
        return jnp.minimum(t * ATTN_QB + ATTN_QB, nb - 1)

    in_specs = [
        pl.BlockSpec(memory_space=pltpu.SMEM),
        pl.BlockSpec((ATTN_TQ, ATTN_W), lambda b, t: (b * nq + t, 0)),
        pl.BlockSpec((ATTN_TQ, 2 * KV_W), lambda b, t: (b * nq + t, 0)),
        pl.BlockSpec((BLOCK, 2 * KV_W), lambda b, t: (b * nb + prev_blk(t), 0)),
        pl.BlockSpec((BLOCK, 2 * KV_W), lambda b, t: (b * nb + next_blk(t), 0)),
        pl.BlockSpec((ATTN_TQ, ATTN_W), lambda b, t: (b * nq + t, 0)),
        pl.BlockSpec((CTX_LEN, 2 * KV_W), lambda b, t: (b, 0)),
        pl.BlockSpec((ATTN_TQ, HEAD_DIM), lambda b, t: (t, 0)),
        pl.BlockSpec((ATTN_TQ, HEAD_DIM), lambda b, t: (t, 0)),
        pl.BlockSpec((BLOCK, HEAD_DIM), lambda b, t: (prev_blk(t), 0)),
        pl.BlockSpec((BLOCK, HEAD_DIM), lambda b, t: (prev_blk(t), 0)),
        pl.BlockSpec((BLOCK, HEAD_DIM), lambda b, t: (next_blk(t), 0)),
        pl.BlockSpec((BLOCK, HEAD_DIM), lambda b, t: (next_blk(t), 0)),
        pl.BlockSpec((1, HEAD_DIM), lambda b, t: (0, 0)),
        pl.BlockSpec((1, HEAD_DIM), lambda b, t: (0, 0)),
    ]
    return pl.pallas_call(
        _lat_attn_kernel,
        grid=(BATCH, nq),
        in_specs=in_specs,
        out_specs=pl.BlockSpec((ATTN_TQ, ATTN_W), lambda b, t: (b * nq + t, 0)),
        out_shape=jax.ShapeDtypeStruct((BATCH * SEQ, ATTN_W), BF16),
        scratch_shapes=[pltpu.VMEM((N_KV_HEADS, ATTN_QB, GROUP * BLOCK, HEAD_DIM), BF16),
                        pltpu.VMEM((N_KV_HEADS, ATTN_TQ + 2 * BLOCK, HEAD_DIM), BF16),
                        pltpu.VMEM((N_KV_HEADS, ATTN_TQ + 2 * BLOCK, HEAD_DIM), BF16),
                        pltpu.VMEM((N_KV_HEADS, CTX_LEN, HEAD_DIM), BF16)],
        compiler_params=_cparams(2),
        name="latent_attention",
    )(sink, q, kv, kv, kv, ag, kv_ctx, cos, sin_s, cos, sin_s, cos, sin_s, q_g, k_g)


def _ctx_attn_kernel(sink_ref, q_ref, kv_ref, ag_ref, qg_ref, kg_ref, o_ref):
    qg = qg_ref[...]
    kg = kg_ref[...]
    for h in range(N_KV_HEADS):
        kx = _rms_head(kv_ref[:, h * HEAD_DIM:(h + 1) * HEAD_DIM].astype(F32), kg).astype(BF16)
        v = kv_ref[:, KV_W + h * HEAD_DIM:KV_W + (h + 1) * HEAD_DIM]
        for j in range(GROUP):
            hh = h * GROUP + j
            cs = slice(hh * HEAD_DIM, (hh + 1) * HEAD_DIM)
            qn = (_rms_head(q_ref[:, cs].astype(F32), qg) * ATTN_SCALE).astype(BF16)
            s = _dot_nt(qn, kx)
            sink = sink_ref[hh]
            m = jnp.maximum(jnp.max(s, axis=-1, keepdims=True), sink)
            p = jnp.exp(s - m)
            den = jnp.sum(p, axis=-1, keepdims=True) + jnp.exp(sink - m)
            o = _dot(p.astype(BF16), v) * (1.0 / den)
            o_ref[:, cs] = (o * ag_ref[:, cs].astype(F32)).astype(o_ref.dtype)


def _context_attention(q, kv, ag, sink, q_g, k_g):
    return pl.pallas_call(
        _ctx_attn_kernel,
        grid=(BATCH,),
        in_specs=[pl.BlockSpec(memory_space=pltpu.SMEM),
                  pl.BlockSpec((CTX_LEN, ATTN_W), lambda b: (b, 0)),
                  pl.BlockSpec((CTX_LEN, 2 * KV_W), lambda b: (b, 0)),
                  pl.BlockSpec((CTX_LEN, ATTN_W), lambda b: (b, 0)),
                  pl.BlockSpec((1, HEAD_DIM), lambda b: (0, 0)),
                  pl.BlockSpec((1, HEAD_DIM), lambda b: (0, 0))],
        out_specs=pl.BlockSpec((CTX_LEN, ATTN_W), lambda b: (b, 0)),
        out_shape=jax.ShapeDtypeStruct((BATCH * CTX_LEN, ATTN_W), BF16),
        compiler_params=_cparams(1),
        name="context_attention",
    )(sink, q, kv, ag, q_g, k_g)


def _table(values):
    return jnp.asarray(values, F32).astype(BF16)


def _channel_dft_tables():
    n = FOURIER_GROUP_W
    ang = 2.0 * np.pi * np.outer(np.arange(n), np.arange(n)) / n
    s = n ** -0.5
    return _table(np.cos(ang) * s), _table(np.sin(ang) * s)


def _stage1_table():
    ang = 2.0 * np.pi * np.outer(np.arange(FFT_N2), np.arange(FFT_N2)) / FFT_N2
    s = FFT_N2 ** -0.5
    return _table(np.concatenate([np.cos(ang), -np.sin(ang)], axis=0) * s)


def _stage2_table():
    k2 = np.arange(FFT_N2)[:, None, None]
    k1 = np.arange(FFT_N1)[None, :, None]
    l1 = np.arange(FFT_N1)[None, None, :]
    ang = 2.0 * np.pi * ((l1 * (k2 + FFT_N2 * k1)) % SEQ) / SEQ
    tr = np.cos(ang) * FFT_N1 ** -0.5
    ti = -np.sin(ang) * FFT_N1 ** -0.5
    top = np.concatenate([tr, -ti], axis=2)
    bot = np.concatenate([ti, tr], axis=2)
    return _table(np.concatenate([top, bot], axis=1))


def _interleave_table():
    n = 8 * FFT_KB
    p = np.zeros((n, n), np.float32)
    for b in range(8):
        for j in range(FFT_KB):
            p[b * FFT_KB + j, j * 8 + b] = 1.0
    return _table(p)


def _fft1_kernel(d_ref, u_ref, o_ref):
    o_ref[...] = _dot(d_ref[...], u_ref[...]).astype(o_ref.dtype)


def _fft2_kernel(z_ref, t_ref, cc_ref, sc_ref, w_ref, p_ref, fg_ref, o_ref, ybuf):
    cc = cc_ref[...]
    sc = sc_ref[...]
    w = w_ref[...]
    for j in range(FFT_KB):
        zst = jnp.concatenate([z_ref[0, j], z_ref[1, j]], axis=0)
        y = _dot(t_ref[j], zst)
        a = _dot(y[:FFT_N1].astype(BF16), cc) + _dot(y[FFT_N1:].astype(BF16), sc)
        ybuf[j * FFT_N1:(j + 1) * FFT_N1, :] = _dot(a.astype(BF16), w)
    perm = p_ref[...]
    for a8 in range(FFT_N1 // 8):
        blk = jnp.concatenate([ybuf[j * FFT_N1 + 8 * a8:j * FFT_N1 + 8 * a8 + 8, :] for j in range(FFT_KB)], axis=0)
        nat = _dot(perm, blk.astype(BF16))
        gate = fg_ref[8 * a8:8 * a8 + 8].reshape(8 * FFT_KB, FOURIER_GROUP_W).astype(F32)
        o_ref[8 * a8:8 * a8 + 8] = (nat * gate).astype(o_ref.dtype).reshape(8, FFT_KB, FOURIER_GROUP_W)


def _latent_fourier(u, fg, w_mix, layer, tabs):
    d1, t2, cc, sc, perm = tabs
    cols = FFT_N1 * FOURIER_W
    tn = 8192
    z = pl.pallas_call(
        _fft1_kernel,
        grid=(BATCH, cols // tn),
        in_specs=[pl.BlockSpec((2 * FFT_N2, FFT_N2), lambda b, j: (0, 0)),
                  pl.BlockSpec((None, FFT_N2, tn), lambda b, j: (b, 0, j))],
        out_specs=pl.BlockSpec((None, 2 * FFT_N2, tn), lambda b, j: (b, 0, j)),
        out_shape=jax.ShapeDtypeStruct((BATCH, 2 * FFT_N2, cols), BF16),
        compiler_params=_cparams(2),
        name="fourier_stage1",
    )(d1, u.reshape(BATCH, FFT_N2, cols))
    z = z.reshape(BATCH, 2, FFT_N2, FFT_N1, FOURIER_W)
    gw = FOURIER_GROUP_W
    out = pl.pallas_call(
        _fft2_kernel,
        grid=(BATCH, FFT_N2 // FFT_KB, FOURIER_GROUPS),
        in_specs=[pl.BlockSpec((None, 2, FFT_KB, FFT_N1, gw), lambda b, t, g: (b, 0, t, 0, g)),
                  pl.BlockSpec((FFT_KB, 2 * FFT_N1, 2 * FFT_N1), lambda b, t, g: (t, 0, 0)),
                  pl.BlockSpec((gw, gw), lambda b, t, g: (0, 0)),
                  pl.BlockSpec((gw, gw), lambda b, t, g: (0, 0)),
                  pl.BlockSpec((None, None, gw, gw), lambda b, t, g: (layer, g, 0, 0)),
                  pl.BlockSpec((8 * FFT_KB, 8 * FFT_KB), lambda b, t, g: (0, 0)),
                  pl.BlockSpec((None, FFT_N1, FFT_KB, gw), lambda b, t, g: (b, 0, t, g))],
        out_specs=pl.BlockSpec((None, FFT_N1, FFT_KB, gw), lambda b, t, g: (b, 0, t, g)),
        out_shape=jax.ShapeDtypeStruct((BATCH, FFT_N1, FFT_N2, FOURIER_W), BF16),
        scratch_shapes=[pltpu.VMEM((FFT_KB * FFT_N1, gw), F32)],
        compiler_params=_cparams(3),
        name="fourier_stage2",
    )(z, t2, cc, sc, w_mix, perm, fg.reshape(BATCH, FFT_N1, FFT_N2, FOURIER_W))
    return out.reshape(BATCH * SEQ, FOURIER_W)


def _ctx_dft_table():
    n = CTX_LEN
    ang = 2.0 * np.pi * np.outer(np.arange(n), np.arange(n)) / n
    s = n ** -0.5
    return _table(np.concatenate([np.cos(ang), -np.sin(ang)], axis=1) * s)


def _ctx_fourier_kernel(u_ref, fg_ref, cls_ref, cc_ref, sc_ref, w_ref, o_ref):
    u = u_ref[...]
    st = jnp.concatenate([_dot(u, cc_ref[...]), _dot(u, sc_ref[...])], axis=0).astype(BF16)
    f = _dot(cls_ref[...], st)
    y = _dot(f.astype(BF16), w_ref[...])
    o_ref[...] = (y * fg_ref[...].astype(F32)).astype(o_ref.dtype)


def _context_fourier(u, fg, w_mix, layer, cls, cc, sc):
    gw = FOURIER_GROUP_W
    return pl.pallas_call(
        _ctx_fourier_kernel,
        grid=(BATCH, FOURIER_GROUPS),
        in_specs=[pl.BlockSpec((CTX_LEN, gw), lambda b, g: (b, g)),
                  pl.BlockSpec((CTX_LEN, gw), lambda b, g: (b, g)),
                  pl.BlockSpec((CTX_LEN, 2 * CTX_LEN), lambda b, g: (0, 0)),
                  pl.BlockSpec((gw, gw), lambda b, g: (0, 0)),
                  pl.BlockSpec((gw, gw), lambda b, g: (0, 0)),
                  pl.BlockSpec((None, None, gw, gw), lambda b, g: (layer, g, 0, 0))],
        out_specs=pl.BlockSpec((CTX_LEN, gw), lambda b, g: (b, g)),
        out_shape=jax.ShapeDtypeStruct((BATCH * CTX_LEN, FOURIER_W), BF16),
        compiler_params=_cparams(2),
        name="context_fourier",
    )(u, fg, cls, cc, sc, w_mix)


def _conv_kernel(a_ref, b_ref, ap_ref, bp_ref, an_ref, bn_ref, cg_ref, dww_ref, dwb_ref, lng_ref, lnb_ref,
                 wpw_ref, o_ref, ubuf, *, tiles_per_seq, tr):
    ti = lax.rem(pl.program_id(0), tiles_per_seq)
    u = a_ref[...].astype(F32) * b_ref[...].astype(F32)
    up = ap_ref[...].astype(F32) * bp_ref[...].astype(F32)
    un = an_ref[...].astype(F32) * bn_ref[...].astype(F32)
    up = jnp.where(ti > 0, up, 0.0)
    un = jnp.where(ti < tiles_per_seq - 1, un, 0.0)
    n_slab = CONV_W // LANES
    for c in range(n_slab):
        cs = slice(c * LANES, (c + 1) * LANES)
        ubuf[c, 0:CONV_HALO, :] = up[:, cs]
        ubuf[c, CONV_HALO:CONV_HALO + tr, :] = u[:, cs]
        ubuf[c, CONV_HALO + tr:2 * CONV_HALO + tr, :] = un[:, cs]
    first = CONV_HALO - CONV_K // 2
    ys = []
    for c in range(n_slab):
        cs = slice(c * LANES, (c + 1) * LANES)
        acc = ubuf[c, first:first + tr, :] * dww_ref[0:1, cs]
        for k in range(1, CONV_K):
            acc = acc + ubuf[c, first + k:first + k + tr, :] * dww_ref[k:k + 1, cs]
        ys.append(acc)
    y = jnp.concatenate(ys, axis=1) + dwb_ref[...]
    mu = jnp.mean(y, axis=-1, keepdims=True)
    yc = y - mu
    var = jnp.mean(yc * yc, axis=-1, keepdims=True)
    z = _silu(yc * lax.rsqrt(var + EPS) * lng_ref[...] + lnb_ref[...])
    out = _dot(z.astype(BF16), wpw_ref[...]) * cg_ref[...].astype(F32)
    o_ref[...] = out.astype(o_ref.dtype)


def _conformer(a, b_sig, cg, dw_w, dw_b, ln_g, ln_b, w_pw, layer, seq_len, tr):
    m = a.shape[0]
    tiles_per_seq = seq_len // tr
    hb = tr // CONV_HALO
    n_hb = m // CONV_HALO

    def prev_blk(i):
        return jnp.maximum(i * hb - 1, 0)

    def next_blk(i):
        return jnp.minimum(i * hb + hb, n_hb - 1)

    tile = pl.BlockSpec((tr, CONV_W), lambda i: (i, 0))
    halo_p = pl.BlockSpec((CONV_HALO, CONV_W), lambda i: (prev_blk(i), 0))
    halo_n = pl.BlockSpec((CONV_HALO, CONV_W), lambda i: (next_blk(i), 0))
    vec = pl.BlockSpec((None, 1, CONV_W), lambda i: (layer, 0, 0))
    return pl.pallas_call(
        functools.partial(_conv_kernel, tiles_per_seq=tiles_per_seq, tr=tr),
        grid=(m // tr,),
        in_specs=[tile, tile, halo_p, halo_p, halo_n, halo_n, tile,
                  pl.BlockSpec((None, CONV_K, CONV_W), lambda i: (layer, 0, 0)),
                  vec, vec, vec,
                  pl.BlockSpec((None, CONV_W, CONV_W), lambda i: (layer, 0, 0))],
        out_specs=tile,
        out_shape=jax.ShapeDtypeStruct((m, CONV_W), BF16),
        scratch_shapes=[pltpu.VMEM((CONV_W // LANES, tr + 2 * CONV_HALO, LANES), F32)],
        compiler_params=_cparams(1),
        name="conformer_conv",
    )(a, b_sig, a, b_sig, a, b_sig, cg, dw_w,
      dw_b.reshape(DEPTH, 1, CONV_W), ln_g.reshape(DEPTH, 1, CONV_W), ln_b.reshape(DEPTH, 1, CONV_W), w_pw)


def _merge_kernel(a_ref, f_ref, c_ref, g0_ref, g1_ref, g2_ref, wa_ref, wf_ref, wc_ref, o_ref):
    merged = (g0_ref[...].astype(F32) * _dot(a_ref[...], wa_ref[...])
              + g1_ref[...].astype(F32) * _dot(f_ref[...], wf_ref[...])
              + g2_ref[...].astype(F32) * _dot(c_ref[...], wc_ref[...]))
    o_ref[...] = merged.astype(o_ref.dtype)


def _merge(attn, four, conv, mg, w_attn_up, w_four_up, w_conv_up, layer, tm):
    m = attn.shape[0]
    tn = 1024
    nj = D_MODEL // tn
    return pl.pallas_call(
        _merge_kernel,
        grid=(nj, m // tm),
        in_specs=[pl.BlockSpec((tm, ATTN_W), lambda j, i: (i, 0)),
                  pl.BlockSpec((tm, FOURIER_W), lambda j, i: (i, 0)),
                  pl.BlockSpec((tm, CONV_W), lambda j, i: (i, 0)),
                  pl.BlockSpec((tm, tn), lambda j, i: (i, j)),
                  pl.BlockSpec((tm, tn), lambda j, i: (i, nj + j)),
                  pl.BlockSpec((tm, tn), lambda j, i: (i, 2 * nj + j)),
                  pl.BlockSpec((None, ATTN_W, tn), lambda j, i: (layer, 0, j)),
                  pl.BlockSpec((None, FOURIER_W, tn), lambda j, i: (layer, 0, j)),
                  pl.BlockSpec((None, CONV_W, tn), lambda j, i: (layer, 0, j))],
        out_specs=pl.BlockSpec((tm, tn), lambda j, i: (i, j)),
        out_shape=jax.ShapeDtypeStruct((m, D_MODEL), BF16),
        compiler_params=_cparams(2),
        name="branch_merge",
    )(attn, four, conv, mg, mg, mg, w_attn_up, w_four_up, w_conv_up)


def _out_kernel(a_ref, w_ref, x_ref, gt_ref, o_ref):
    o_ref[...] = x_ref[...] + gt_ref[...] * _dot(a_ref[...], w_ref[...])


def _out_proj(merged, w_out, x2d, mods, layer, row_of_tile, tm):
    m = merged.shape[0]
    tn = 1024
    return pl.pallas_call(
        _out_kernel,
        grid=(D_MODEL // tn, m // tm),
        in_specs=[pl.BlockSpec((tm, D_MODEL), lambda j, i: (i, 0)),
                  pl.BlockSpec((None, D_MODEL, tn), lambda j, i: (layer, 0, j)),
                  pl.BlockSpec((tm, tn), lambda j, i: (i, j)),
                  pl.BlockSpec((None, None, 1, tn), lambda j, i: (layer, row_of_tile(i), 0, 2 * (D_MODEL // tn) + j))],
        out_specs=pl.BlockSpec((tm, tn), lambda j, i: (i, j)),
        out_shape=jax.ShapeDtypeStruct((m, D_MODEL), F32),
        compiler_params=_cparams(2),
        name="out_proj",
    )(merged, w_out, x2d, mods)


def _branches_and_merge(h, w, layer, tm, attention, fourier, conv_seq, conv_tr):
    w_in = w["w_in"]
    q = _proj(h, w_in, layer, Q_OFF, ATTN_W, "none", tm)
    kv = _proj(h, w_in, layer, K_OFF, 2 * KV_W, "none", tm)
    ag = _proj(h, w_in, layer, AG_OFF, ATTN_W, "silu", tm)
    fu = _proj(h, w_in, layer, F_OFF, FOURIER_W, "none", tm)
    fg = _proj(h, w_in, layer, FG_OFF, FOURIER_W, "silu", tm)
    ca = _proj(h, w_in, layer, CA_OFF, CONV_W, "none", tm)
    cb = _proj(h, w_in, layer, CB_OFF, CONV_W, "sigmoid", tm)
    cg = _proj(h, w_in, layer, CG_OFF, CONV_W, "silu", tm)
    mg = _proj(h, w_in, layer, MG_OFF, N_BRANCH * D_MODEL, "sigmoid", tm)
    attn = attention(q, kv, ag)
    four = fourier(fu, fg)
    conv = _conformer(ca, cb, cg, w["conv_dw_w"], w["conv_dw_b"], w["conv_ln_g"], w["conv_ln_b"], w["w_conv_pw"],
                      layer, conv_seq, conv_tr)
    merged = _merge(attn, four, conv, mg, w["w_attn_up"], w["w_fourier_up"], w["w_conv_up"], layer, min(tm, 512))
    return merged, kv


@jax.jit
def _block(x, c, ctx, c_ctx, norm_g, w_mod, b_mod, w_in, q_norm_g, k_norm_g, attn_sink, w_attn_up,
           w_fourier_mix, w_fourier_up, conv_dw_w, conv_dw_b, conv_ln_g, conv_ln_b, w_conv_pw, w_conv_up, w_out):
    w = dict(w_in=w_in.astype(BF16), w_attn_up=w_attn_up.astype(BF16), w_fourier_up=w_fourier_up.astype(BF16),
             w_conv_pw=w_conv_pw.astype(BF16), w_conv_up=w_conv_up.astype(BF16),
             conv_dw_w=conv_dw_w, conv_dw_b=conv_dw_b, conv_ln_g=conv_ln_g, conv_ln_b=conv_ln_b)
    w_out_b = w_out.astype(BF16)
    w_mix_b = w_fourier_mix.astype(BF16)

    cos, sin_s = _rope_tables()
    cc_t, sc_t = _channel_dft_tables()
    lat_tabs = (_stage1_table(), _stage2_table(), cc_t, sc_t, _interleave_table())
    cls_t = _ctx_dft_table()

    cond = jnp.concatenate([c, c_ctx[None, :], jnp.zeros((8 - BATCH - 1, D_MODEL), F32)], axis=0)
    mods = _modulation(cond, w_mod, b_mod).reshape(DEPTH, 8, 1, 3 * D_MODEL)
    ctx_row = BATCH

    x2d = x.reshape(BATCH * SEQ, D_MODEL)
    ctx2d = ctx.reshape(BATCH * CTX_LEN, D_MODEL)
    lat_tm = 1024
    ctx_tm = BATCH * CTX_LEN
    norm_tm = 256
    lat_tiles_per_batch = SEQ // norm_tm

    for l in range(DEPTH):
        sink = attn_sink[l]
        qg = q_norm_g[l].reshape(1, HEAD_DIM)
        kg = k_norm_g[l].reshape(1, HEAD_DIM)

        h_ctx = _adaln(ctx2d, norm_g, mods, l, lambda i: ctx_row, norm_tm)
        if l < DEPTH - 1:
            merged_c, kv_c = _branches_and_merge(
                h_ctx, w, l, ctx_tm,
                attention=lambda q, kv, ag: _context_attention(q, kv, ag, sink, qg, kg),
                fourier=lambda fu, fg: _context_fourier(fu, fg, w_mix_b, l, cls_t, cc_t, sc_t),
                conv_seq=CTX_LEN, conv_tr=CTX_LEN)
            ctx_next = _out_proj(merged_c, w_out_b, ctx2d, mods, l, lambda i: ctx_row, ctx_tm)
        else:
            kv_c = _proj(h_ctx, w["w_in"], l, K_OFF, 2 * KV_W, "none", ctx_tm)
            ctx_next = ctx2d

        h = _adaln(x2d, norm_g, mods, l, lambda i: i // lat_tiles_per_batch, norm_tm)
        merged, _ = _branches_and_merge(
            h, w, l, lat_tm,
            attention=lambda q, kv, ag: _latent_attention(q, kv, ag, kv_c, sink, qg, kg, cos, sin_s),
            fourier=lambda fu, fg: _latent_fourier(fu, fg, w_mix_b, l, lat_tabs),
            conv_seq=SEQ, conv_tr=256)
        out_tm = 512
        x2d = _out_proj(merged, w_out_b, x2d, mods, l, lambda i: i // (SEQ // out_tm), out_tm)
        ctx2d = ctx_next
    return x2d.reshape(BATCH, SEQ, D_MODEL)


def kernel(x, c, ctx, c_ctx, norm_g, w_mod, b_mod, w_in, q_norm_g, k_norm_g, attn_sink, w_attn_up, w_fourier_mix,
           w_fourier_up, conv_dw_w, conv_dw_b, conv_ln_g, conv_ln_b, w_conv_pw, w_conv_up, w_out):
    return _block(x, c, ctx, c_ctx, norm_g, w_mod, b_mod, w_in, q_norm_g, k_norm_g, attn_sink, w_attn_up,
                  w_fourier_mix, w_fourier_up, conv_dw_w, conv_dw_b, conv_ln_g, conv_ln_b, w_conv_pw, w_conv_up,
                  w_out)
```

```python
import functools

import numpy as np
import jax
import jax.numpy as jnp
from jax import lax
from jax.experimental import pallas as pl
from jax.experimental.pallas import tpu as pltpu

D_MODEL = 4096
BATCH = 2
SEQ = 8192
DEPTH = 2
CTX_LEN = 256
GRID_W = 64

N_HEADS = 16
N_KV_HEADS = 4
HEAD_DIM = 128
GROUP = N_HEADS // N_KV_HEADS
ATTN_W = N_HEADS * HEAD_DIM
KV_W = N_KV_HEADS * HEAD_DIM
WINDOW = 128
BLOCK = 128
ROPE_BASE = 10000.0

FOURIER_GROUPS = 4
FOURIER_GROUP_W = 256
FOURIER_W = FOURIER_GROUPS * FOURIER_GROUP_W

CONV_W = 1024
CONV_K = 31
CONV_HALO = 16

N_BRANCH = 3
EPS = 1e-6
NEG_INF = -1e30

Q_OFF = 0
K_OFF = Q_OFF + ATTN_W
AG_OFF = K_OFF + 2 * KV_W
F_OFF = AG_OFF + ATTN_W
FG_OFF = F_OFF + FOURIER_W
CA_OFF = FG_OFF + FOURIER_W
CB_OFF = CA_OFF + CONV_W
CG_OFF = CB_OFF + CONV_W
MG_OFF = CG_OFF + CONV_W
IN_W = MG_OFF + N_BRANCH * D_MODEL

FFT_N1 = 128
FFT_N2 = 64
FFT_KB = 16

V7X_VMEM_LIMIT = 56 * 1024 * 1024
LANES = 128

BF16 = jnp.bfloat16
F32 = jnp.float32


def _cparams(n_axes):
    return pltpu.CompilerParams(dimension_semantics=("arbitrary",) * n_axes,
                                vmem_limit_bytes=V7X_VMEM_LIMIT)


def _silu(v):
    return v * jax.nn.sigmoid(v)


def _act(v, act):
    if act == "silu":
        return _silu(v)
    if act == "sigmoid":
        return jax.nn.sigmoid(v)
    return v


def _dot(a, b):
    return jnp.dot(a, b, preferred_element_type=F32)


def _dot_nt(a, b):
    return lax.dot_general(a, b, (((1,), (1,)), ((), ())), preferred_element_type=F32)


def _mod_kernel(c_ref, w_ref, b_ref, o_ref):
    s = _silu(c_ref[...]).astype(BF16)
    o_ref[...] = _dot(s, w_ref[...].astype(BF16)) + b_ref[...]


def _modulation(cc, w_mod, b_mod):
    tn = 512
    return pl.pallas_call(
        _mod_kernel,
        grid=(DEPTH, 3 * D_MODEL // tn),
        in_specs=[pl.BlockSpec((8, D_MODEL), lambda l, j: (0, 0)),
                  pl.BlockSpec((None, D_MODEL, tn), lambda l, j: (l, 0, j)),
                  pl.BlockSpec((None, 1, tn), lambda l, j: (l, 0, j))],
        out_specs=pl.BlockSpec((None, 8, tn), lambda l, j: (l, 0, j)),
        out_shape=jax.ShapeDtypeStruct((DEPTH, 8, 3 * D_MODEL), F32),
        compiler_params=_cparams(2),
        name="modulation",
    )(cc, w_mod, b_mod.reshape(DEPTH, 1, 3 * D_MODEL))


def _adaln_kernel(x_ref, g_ref, sh_ref, sc_ref, o_ref):
    x = x_ref[...]
    ms = jnp.mean(x * x, axis=-1, keepdims=True)
    y = x * lax.rsqrt(ms + EPS) * g_ref[...]
    o_ref[...] = (y * (1.0 + sc_ref[...]) + sh_ref[...]).astype(o_ref.dtype)


def _adaln(x2d, norm_g, mods, layer, row_of_tile, tm):
    m = x2d.shape[0]
    return pl.pallas_call(
        _adaln_kernel,
        grid=(m // tm,),
        in_specs=[pl.BlockSpec((tm, D_MODEL), lambda i: (i, 0)),
                  pl.BlockSpec((None, 1, D_MODEL), lambda i: (layer, 0, 0)),
                  pl.BlockSpec((None, None, 1, D_MODEL), lambda i: (layer, row_of_tile(i), 0, 0)),
                  pl.BlockSpec((None, None, 1, D_MODEL), lambda i: (layer, row_of_tile(i), 0, 1))],
        out_specs=pl.BlockSpec((tm, D_MODEL), lambda i: (i, 0)),
        out_shape=jax.ShapeDtypeStruct((m, D_MODEL), BF16),
        compiler_params=_cparams(1),
        name="adaln",
    )(x2d, norm_g.reshape(DEPTH, 1, D_MODEL), mods, mods)


def _proj_kernel(a_ref, w_ref, o_ref, *, act):
    o_ref[...] = _act(_dot(a_ref[...], w_ref[...]), act).astype(o_ref.dtype)


def _proj(h, w_in, layer, col_off, n_cols, act, tm):
    m, k = h.shape
    tn = 1024
    return pl.pallas_call(
        functools.partial(_proj_kernel, act=act),
        grid=(n_cols // tn, m // tm),
        in_specs=[pl.BlockSpec((tm, k), lambda j, i: (i, 0)),
                  pl.BlockSpec((None, k, tn), lambda j, i: (layer, 0, col_off // tn + j))],
        out_specs=pl.BlockSpec((tm, tn), lambda j, i: (i, j)),
        out_shape=jax.ShapeDtypeStruct((m, n_cols), BF16),
        compiler_params=_cparams(2),
        name="in_proj",
    )(h, w_in)


def _rms_head(x, g):
    ms = jnp.mean(x * x, axis=-1, keepdims=True)
    return x * lax.rsqrt(ms + EPS) * g


def _rope(y, cos, sin_signed):
    lane = lax.broadcasted_iota(jnp.int32, y.shape, 1)
    partner = jnp.where((lane & 32) == 0, pltpu.roll(y, 96, 1), pltpu.roll(y, 32, 1))
    return y * cos + partner * sin_signed


def _rope_tables():
    rows = SEQ // GRID_W
    row = np.repeat(np.arange(rows, dtype=np.float64), GRID_W)
    col = np.tile(np.arange(GRID_W, dtype=np.float64), rows)
    axis_dim = HEAD_DIM // 2
    inv_freq = ROPE_BASE ** (-np.arange(0, axis_dim, 2, dtype=np.float64) / axis_dim)
    ang_r = row[:, None] * inv_freq[None, :]
    ang_c = col[:, None] * inv_freq[None, :]
    ang = np.concatenate([ang_r, ang_r, ang_c, ang_c], axis=-1)
    sign = np.where((np.arange(HEAD_DIM) & 32) == 0, -1.0, 1.0)
    return jnp.asarray(np.concatenate([np.cos(ang), np.sin(ang) * sign[None, :]], axis=1), F32)


ATTN_SCALE = HEAD_DIM ** -0.5
GROUP_W = GROUP * HEAD_DIM
N_BLOCKS = SEQ // BLOCK
MG_TM = 1024
MG_TN = 768
MG_PARTS = 4
UNITS_PER_STEP = 2
assert (MG_TM // BLOCK) * N_KV_HEADS == UNITS_PER_STEP * (N_BRANCH * D_MODEL // MG_TN)


def _attn_scores(q_ref, kp, kc, kn, kx, tp, tc, tn, qg, kg, first, last):
    def kprep(k, t):
        return _rope(_rms_head(k.astype(F32), kg), t[:, :HEAD_DIM], t[:, HEAD_DIM:]).astype(BF16)

    kp, kc, kn = kprep(kp, tp), kprep(kc, tc), kprep(kn, tn)
    kx = _rms_head(kx.astype(F32), kg).astype(BF16)
    q = jnp.concatenate(
        [(_rope(_rms_head(q_ref[:, j * HEAD_DIM:(j + 1) * HEAD_DIM].astype(F32), qg), tc[:, :HEAD_DIM], tc[:, HEAD_DIM:])
          * ATTN_SCALE).astype(BF16) for j in range(GROUP)], axis=0)
    rows = GROUP * BLOCK
    qi = lax.broadcasted_iota(jnp.int32, (rows, BLOCK), 0) & (BLOCK - 1)
    col = lax.broadcasted_iota(jnp.int32, (rows, BLOCK), 1)
    s_p = jnp.where(col >= qi + jnp.where(first, BLOCK, 0), _dot_nt(q, kp), NEG_INF)
    s_c = _dot_nt(q, kc)
    s_n = jnp.where(col <= qi - jnp.where(last, BLOCK, 0), _dot_nt(q, kn), NEG_INF)
    s_x = _dot_nt(q, kx)
    return s_p, s_c, s_n, s_x


def _attn_finish(scores, vp, vc, vn, vx, ag_ref, sinks, o_ref):
    s_p, s_c, s_n, s_x = scores
    rows = GROUP * BLOCK
    rgrp = lax.broadcasted_iota(jnp.int32, (rows, 1), 0) >> 7
    sink = jnp.full((rows, 1), sinks[0], F32)
    for j in range(1, GROUP):
        sink = jnp.where(rgrp == j, sinks[j], sink)
    mx = jnp.maximum(jnp.maximum(s_p, s_c), s_n)
    for c in range(CTX_LEN // BLOCK):
        mx = jnp.maximum(mx, s_x[:, c * BLOCK:(c + 1) * BLOCK])
    m = jnp.maximum(jnp.max(mx, axis=-1, keepdims=True), sink)

    def pv(s, v):
        p = jnp.exp((s - m).astype(BF16))
        return _dot(p, jnp.concatenate([v, jnp.ones_like(v)], axis=1))

    acc = pv(s_p, vp) + pv(s_c, vc) + pv(s_n, vn) + pv(s_x, vx)
    den = acc[:, HEAD_DIM:HEAD_DIM + 1] + jnp.exp(sink - m)
    o = acc[:, :HEAD_DIM] * (1.0 / den)
    for j in range(GROUP):
        cs = slice(j * HEAD_DIM, (j + 1) * HEAD_DIM)
        o_ref[:, cs] = (o[j * BLOCK:(j + 1) * BLOCK] * ag_ref[:, cs].astype(F32)).astype(o_ref.dtype)


def _gate_attn_kernel(sink_ref, h_ref, w_ref, qg_ref, kg_ref, q_ref, ag_ref, kp_ref, kc_ref, kn_ref,
                      vp_ref, vc_ref, vn_ref, kx_ref, vx_ref, tp_ref, tc_ref, tn_ref, mg_ref, o_ref):
    i = pl.program_id(0)
    j = pl.program_id(1)
    blk = lax.rem(i, SEQ // MG_TM) * (MG_TM // BLOCK) + j // UNITS_PER_STEP
    first = blk == 0
    last = blk == N_BLOCKS - 1
    qg = qg_ref[...]
    kg = kg_ref[...]
    tp, tc, tn = tp_ref[...], tc_ref[...], tn_ref[...]

    def gate_rows(part):
        rs = slice(part * MG_TM // MG_PARTS, (part + 1) * MG_TM // MG_PARTS)
        mg_ref[rs, :] = jax.nn.sigmoid(_dot(h_ref[rs, :], w_ref[...])).astype(mg_ref.dtype)

    gate_rows(0)
    scores = []
    for u in range(UNITS_PER_STEP):
        hs = slice(u * HEAD_DIM, (u + 1) * HEAD_DIM)
        scores.append(_attn_scores(q_ref.at[:, u * GROUP_W:(u + 1) * GROUP_W], kp_ref[:, hs], kc_ref[:, hs],
                                   kn_ref[:, hs], kx_ref[:, hs], tp, tc, tn, qg, kg, first, last))
    gate_rows(1)
    for u in range(UNITS_PER_STEP):
        grp = lax.rem(j, N_KV_HEADS // UNITS_PER_STEP) * UNITS_PER_STEP + u
        hs = slice(u * HEAD_DIM, (u + 1) * HEAD_DIM)
        sinks = [sink_ref[grp * GROUP + g] for g in range(GROUP)]
        _attn_finish(scores[u], vp_ref[:, hs], vc_ref[:, hs], vn_ref[:, hs], vx_ref[:, hs],
                     ag_ref.at[:, u * GROUP_W:(u + 1) * GROUP_W], sinks, o_ref.at[u])
    for part in range(2, MG_PARTS):
        gate_rows(part)


def _gates_and_attention(h, w_mg, layer, q, kv, ag, kv_ctx, sink, q_g, k_g, rope_tab):
    m = h.shape[0]
    upb = UNITS_PER_STEP
    pairs = N_KV_HEADS // upb
    blocks_per_tile = MG_TM // BLOCK
    tiles_per_seq = SEQ // MG_TM

    def blk(i, j):
        return lax.rem(i, tiles_per_seq) * blocks_per_tile + j // upb

    def seq0(i):
        return (i // tiles_per_seq) * N_BLOCKS

    def cur(i, j):
        return seq0(i) + blk(i, j)

    def prev(i, j):
        return seq0(i) + jnp.maximum(blk(i, j) - 1, 0)

    def nxt(i, j):
        return seq0(i) + jnp.minimum(blk(i, j) + 1, N_BLOCKS - 1)

    def pair(j):
        return lax.rem(j, pairs)

    kw = upb * HEAD_DIM
    k_spec = lambda rowf: pl.BlockSpec((BLOCK, kw), lambda i, j: (rowf(i, j), pair(j)))
    v_spec = lambda rowf: pl.BlockSpec((BLOCK, kw), lambda i, j: (rowf(i, j), pairs + pair(j)))
    t_spec = lambda rowf: pl.BlockSpec((BLOCK, 2 * HEAD_DIM), lambda i, j: (rowf(i, j) - seq0(i), 0))
    in_specs = [
        pl.BlockSpec(memory_space=pltpu.SMEM),
        pl.BlockSpec((MG_TM, D_MODEL), lambda i, j: (i, 0)),
        pl.BlockSpec((None, D_MODEL, MG_TN), lambda i, j: (layer, 0, j)),
        pl.BlockSpec((1, HEAD_DIM), lambda i, j: (0, 0)),
        pl.BlockSpec((1, HEAD_DIM), lambda i, j: (0, 0)),
        pl.BlockSpec((BLOCK, upb * GROUP_W), lambda i, j: (cur(i, j), pair(j))),
        pl.BlockSpec((BLOCK, upb * GROUP_W), lambda i, j: (cur(i, j), pair(j))),
        k_spec(prev), k_spec(cur), k_spec(nxt),
        v_spec(prev), v_spec(cur), v_spec(nxt),
        pl.BlockSpec((CTX_LEN, kw), lambda i, j: (i // tiles_per_seq, pair(j))),
        pl.BlockSpec((CTX_LEN, kw), lambda i, j: (i // tiles_per_seq, pairs + pair(j))),
        t_spec(prev), t_spec(cur), t_spec(nxt),
    ]
    out_specs = [pl.BlockSpec((MG_TM, MG_TN), lambda i, j: (i, j)),
                 pl.BlockSpec((upb, BLOCK, GROUP_W), lambda i, j: (pair(j), cur(i, j), 0))]
    return pl.pallas_call(
        _gate_attn_kernel,
        grid=(m // MG_TM, N_BRANCH * D_MODEL // MG_TN),
        in_specs=in_specs,
        out_specs=out_specs,
        out_shape=[jax.ShapeDtypeStruct((m, N_BRANCH * D_MODEL), BF16),
                   jax.ShapeDtypeStruct((N_KV_HEADS, m, GROUP_W), BF16)],
        compiler_params=_cparams(2),
        name="gates_and_attention",
    )(sink, h, w_mg, q_g, k_g, q, ag, kv, kv, kv, kv, kv, kv, kv_ctx, kv_ctx, rope_tab, rope_tab, rope_tab)


def _ctx_attn_kernel(sink_ref, q_ref, kv_ref, ag_ref, qg_ref, kg_ref, o_ref):
    qg = qg_ref[...]
    kg = kg_ref[...]
    for h in range(N_KV_HEADS):
        kx = _rms_head(kv_ref[:, h * HEAD_DIM:(h + 1) * HEAD_DIM].astype(F32), kg).astype(BF16)
        v = kv_ref[:, KV_W + h * HEAD_DIM:KV_W + (h + 1) * HEAD_DIM]
        for j in range(GROUP):
            hh = h * GROUP + j
            cs = slice(hh * HEAD_DIM, (hh + 1) * HEAD_DIM)
            qn = (_rms_head(q_ref[:, cs].astype(F32), qg) * ATTN_SCALE).astype(BF16)
            s = _dot_nt(qn, kx)
            sink = sink_ref[hh]
            m = jnp.maximum(jnp.max(s, axis=-1, keepdims=True), sink)
            p = jnp.exp(s - m)
            den = jnp.sum(p, axis=-1, keepdims=True) + jnp.exp(sink - m)
            o = _dot(p.astype(BF16), v) * (1.0 / den)
            o_ref[h, :, j * HEAD_DIM:(j + 1) * HEAD_DIM] = (o * ag_ref[:, cs].astype(F32)).astype(o_ref.dtype)


def _context_attention(q, kv, ag, sink, q_g, k_g):
    return pl.pallas_call(
        _ctx_attn_kernel,
        grid=(BATCH,),
        in_specs=[pl.BlockSpec(memory_space=pltpu.SMEM),
                  pl.BlockSpec((CTX_LEN, ATTN_W), lambda b: (b, 0)),
                  pl.BlockSpec((CTX_LEN, 2 * KV_W), lambda b: (b, 0)),
                  pl.BlockSpec((CTX_LEN, ATTN_W), lambda b: (b, 0)),
                  pl.BlockSpec((1, HEAD_DIM), lambda b: (0, 0)),
                  pl.BlockSpec((1, HEAD_DIM), lambda b: (0, 0))],
        out_specs=pl.BlockSpec((N_KV_HEADS, CTX_LEN, GROUP_W), lambda b: (0, b, 0)),
        out_shape=jax.ShapeDtypeStruct((N_KV_HEADS, BATCH * CTX_LEN, GROUP_W), BF16),
        compiler_params=_cparams(1),
        name="context_attention",
    )(sink, q, kv, ag, q_g, k_g)


def _table(values):
    return jnp.asarray(values, F32).astype(BF16)


def _channel_dft_tables():
    n = FOURIER_GROUP_W
    ang = 2.0 * np.pi * np.outer(np.arange(n), np.arange(n)) / n
    s = n ** -0.5
    return _table(np.cos(ang) * s), _table(np.sin(ang) * s)


def _stage1_table():
    ang = 2.0 * np.pi * np.outer(np.arange(FFT_N2), np.arange(FFT_N2)) / FFT_N2
    s = FFT_N2 ** -0.5
    return _table(np.concatenate([np.cos(ang), -np.sin(ang)], axis=0) * s)


def _stage2_table():
    k2 = np.arange(FFT_N2)[:, None, None]
    k1 = np.arange(FFT_N1)[None, :, None]
    l1 = np.arange(FFT_N1)[None, None, :]
    ang = 2.0 * np.pi * ((l1 * (k2 + FFT_N2 * k1)) % SEQ) / SEQ
    tr = np.cos(ang) * FFT_N1 ** -0.5
    ti = -np.sin(ang) * FFT_N1 ** -0.5
    top = np.concatenate([tr, -ti], axis=2)
    bot = np.concatenate([ti, tr], axis=2)
    return _table(np.concatenate([top, bot], axis=1))


def _interleave_table():
    n = 8 * FFT_KB
    p = np.zeros((n, n), np.float32)
    for b in range(8):
        for j in range(FFT_KB):
            p[b * FFT_KB + j, j * 8 + b] = 1.0
    return _table(p)


def _fft1_kernel(d_ref, u_ref, o_ref):
    o_ref[...] = _dot(d_ref[...], u_ref[...]).astype(o_ref.dtype)


def _fft2_kernel(z_ref, t_ref, cc_ref, sc_ref, w_ref, p_ref, fg_ref, o_ref, ybuf):
    cc = cc_ref[...]
    sc = sc_ref[...]
    w = w_ref[...]
    for j in range(FFT_KB):
        zst = jnp.concatenate([z_ref[0, j], z_ref[1, j]], axis=0)
        y = _dot(t_ref[j], zst)
        a = _dot(y[:FFT_N1].astype(BF16), cc) + _dot(y[FFT_N1:].astype(BF16), sc)
        ybuf[j * FFT_N1:(j + 1) * FFT_N1, :] = _dot(a.astype(BF16), w)
    perm = p_ref[...]
    for a8 in range(FFT_N1 // 8):
        blk = jnp.concatenate([ybuf[j * FFT_N1 + 8 * a8:j * FFT_N1 + 8 * a8 + 8, :] for j in range(FFT_KB)], axis=0)
        nat = _dot(perm, blk.astype(BF16))
        gate = fg_ref[8 * a8:8 * a8 + 8].reshape(8 * FFT_KB, FOURIER_GROUP_W).astype(F32)
        o_ref[8 * a8:8 * a8 + 8] = (nat * gate).astype(o_ref.dtype).reshape(8, FFT_KB, FOURIER_GROUP_W)


def _latent_fourier(u, fg, w_mix, layer, tabs):
    d1, t2, cc, sc, perm = tabs
    cols = FFT_N1 * FOURIER_W
    tn = 8192
    z = pl.pallas_call(
        _fft1_kernel,
        grid=(BATCH, cols // tn),
        in_specs=[pl.BlockSpec((2 * FFT_N2, FFT_N2), lambda b, j: (0, 0)),
                  pl.BlockSpec((None, FFT_N2, tn), lambda b, j: (b, 0, j))],
        out_specs=pl.BlockSpec((None, 2 * FFT_N2, tn), lambda b, j: (b, 0, j)),
        out_shape=jax.ShapeDtypeStruct((BATCH, 2 * FFT_N2, cols), BF16),
        compiler_params=_cparams(2),
        name="fourier_stage1",
    )(d1, u.reshape(BATCH, FFT_N2, cols))
    z = z.reshape(BATCH, 2, FFT_N2, FFT_N1, FOURIER_W)
    gw = FOURIER_GROUP_W
    out = pl.pallas_call(
        _fft2_kernel,
        grid=(BATCH, FFT_N2 // FFT_KB, FOURIER_GROUPS),
        in_specs=[pl.BlockSpec((None, 2, FFT_KB, FFT_N1, gw), lambda b, t, g: (b, 0, t, 0, g)),
                  pl.BlockSpec((FFT_KB, 2 * FFT_N1, 2 * FFT_N1), lambda b, t, g: (t, 0, 0)),
                  pl.BlockSpec((gw, gw), lambda b, t, g: (0, 0)),
                  pl.BlockSpec((gw, gw), lambda b, t, g: (0, 0)),
                  pl.BlockSpec((None, None, gw, gw), lambda b, t, g: (layer, g, 0, 0)),
                  pl.BlockSpec((8 * FFT_KB, 8 * FFT_KB), lambda b, t, g: (0, 0)),
                  pl.BlockSpec((None, FFT_N1, FFT_KB, gw), lambda b, t, g: (b, 0, t, g))],
        out_specs=pl.BlockSpec((None, FFT_N1, FFT_KB, gw), lambda b, t, g: (b, 0, t, g)),
        out_shape=jax.ShapeDtypeStruct((BATCH, FFT_N1, FFT_N2, FOURIER_W), BF16),
        scratch_shapes=[pltpu.VMEM((FFT_KB * FFT_N1, gw), F32)],
        compiler_params=_cparams(3),
        name="fourier_stage2",
    )(z, t2, cc, sc, w_mix, perm, fg.reshape(BATCH, FFT_N1, FFT_N2, FOURIER_W))
    return out.reshape(BATCH * SEQ, FOURIER_W)


def _ctx_dft_table():
    n = CTX_LEN
    ang = 2.0 * np.pi * np.outer(np.arange(n), np.arange(n)) / n
    s = n ** -0.5
    return _table(np.concatenate([np.cos(ang), -np.sin(ang)], axis=1) * s)


def _ctx_fourier_kernel(u_ref, fg_ref, cls_ref, cc_ref, sc_ref, w_ref, o_ref):
    u = u_ref[...]
    st = jnp.concatenate([_dot(u, cc_ref[...]), _dot(u, sc_ref[...])], axis=0).astype(BF16)
    f = _dot(cls_ref[...], st)
    y = _dot(f.astype(BF16), w_ref[...])
    o_ref[...] = (y * fg_ref[...].astype(F32)).astype(o_ref.dtype)


def _context_fourier(u, fg, w_mix, layer, cls, cc, sc):
    gw = FOURIER_GROUP_W
    return pl.pallas_call(
        _ctx_fourier_kernel,
        grid=(BATCH, FOURIER_GROUPS),
        in_specs=[pl.BlockSpec((CTX_LEN, gw), lambda b, g: (b, g)),
                  pl.BlockSpec((CTX_LEN, gw), lambda b, g: (b, g)),
                  pl.BlockSpec((CTX_LEN, 2 * CTX_LEN), lambda b, g: (0, 0)),
                  pl.BlockSpec((gw, gw), lambda b, g: (0, 0)),
                  pl.BlockSpec((gw, gw), lambda b, g: (0, 0)),
                  pl.BlockSpec((None, None, gw, gw), lambda b, g: (layer, g, 0, 0))],
        out_specs=pl.BlockSpec((CTX_LEN, gw), lambda b, g: (b, g)),
        out_shape=jax.ShapeDtypeStruct((BATCH * CTX_LEN, FOURIER_W), BF16),
        compiler_params=_cparams(2),
        name="context_fourier",
    )(u, fg, cls, cc, sc, w_mix)


def _conv_norm_act(ti, a_ref, b_ref, ap_ref, bp_ref, an_ref, bn_ref, dww_ref, dwb_ref, lng_ref, lnb_ref, ubuf,
                   tiles_per_seq, tr):
    u = a_ref[...].astype(F32) * b_ref[...].astype(F32)
    up = ap_ref[...].astype(F32) * bp_ref[...].astype(F32)
    un = an_ref[...].astype(F32) * bn_ref[...].astype(F32)
    up = jnp.where(ti > 0, up, 0.0)
    un = jnp.where(ti < tiles_per_seq - 1, un, 0.0)
    n_slab = CONV_W // LANES
    for c in range(n_slab):
        cs = slice(c * LANES, (c + 1) * LANES)
        ubuf[c, 0:CONV_HALO, :] = up[:, cs]
        ubuf[c, CONV_HALO:CONV_HALO + tr, :] = u[:, cs]
        ubuf[c, CONV_HALO + tr:2 * CONV_HALO + tr, :] = un[:, cs]
    first = CONV_HALO - CONV_K // 2
    ys = []
    for c in range(n_slab):
        cs = slice(c * LANES, (c + 1) * LANES)
        acc = ubuf[c, first:first + tr, :] * dww_ref[0:1, cs]
        for k in range(1, CONV_K):
            acc = acc + ubuf[c, first + k:first + k + tr, :] * dww_ref[k:k + 1, cs]
        ys.append(acc)
    y = jnp.concatenate(ys, axis=1) + dwb_ref[...]
    mu = jnp.mean(y, axis=-1, keepdims=True)
    yc = y - mu
    var = jnp.mean(yc * yc, axis=-1, keepdims=True)
    return _silu(yc * lax.rsqrt(var + EPS) * lng_ref[...] + lnb_ref[...]).astype(BF16)


def _conv_kernel(a_ref, b_ref, ap_ref, bp_ref, an_ref, bn_ref, cg_ref, dww_ref, dwb_ref, lng_ref, lnb_ref,
                 wpw_ref, o_ref, ubuf, *, tiles_per_seq, tr):
    ti = lax.rem(pl.program_id(0), tiles_per_seq)
    z = _conv_norm_act(ti, a_ref, b_ref, ap_ref, bp_ref, an_ref, bn_ref, dww_ref, dwb_ref, lng_ref, lnb_ref, ubuf,
                       tiles_per_seq, tr)
    o_ref[...] = (_dot(z, wpw_ref[...]) * cg_ref[...].astype(F32)).astype(o_ref.dtype)


def _conv_specs(m, tr, layer, tile_of):
    hb = tr // CONV_HALO
    n_hb = m // CONV_HALO
    tile = pl.BlockSpec((tr, CONV_W), lambda *idx: (tile_of(*idx), 0))
    halo_p = pl.BlockSpec((CONV_HALO, CONV_W), lambda *idx: (jnp.maximum(tile_of(*idx) * hb - 1, 0), 0))
    halo_n = pl.BlockSpec((CONV_HALO, CONV_W), lambda *idx: (jnp.minimum(tile_of(*idx) * hb + hb, n_hb - 1), 0))
    vec = pl.BlockSpec((None, 1, CONV_W), lambda *idx: (layer, 0, 0))
    specs = [tile, tile, halo_p, halo_p, halo_n, halo_n, tile,
             pl.BlockSpec((None, CONV_K, CONV_W), lambda *idx: (layer, 0, 0)),
             vec, vec, vec,
             pl.BlockSpec((None, CONV_W, CONV_W), lambda *idx: (layer, 0, 0))]
    return specs, tile


def _conv_args(a, b_sig, cg, dw_w, dw_b, ln_g, ln_b, w_pw):
    return (a, b_sig, a, b_sig, a, b_sig, cg, dw_w,
            dw_b.reshape(DEPTH, 1, CONV_W), ln_g.reshape(DEPTH, 1, CONV_W), ln_b.reshape(DEPTH, 1, CONV_W), w_pw)


def _conformer(a, b_sig, cg, dw_w, dw_b, ln_g, ln_b, w_pw, layer, seq_len, tr):
    m = a.shape[0]
    specs, tile = _conv_specs(m, tr, layer, lambda i: i)
    return pl.pallas_call(
        functools.partial(_conv_kernel, tiles_per_seq=seq_len // tr, tr=tr),
        grid=(m // tr,),
        in_specs=specs,
        out_specs=tile,
        out_shape=jax.ShapeDtypeStruct((m, CONV_W), BF16),
        scratch_shapes=[pltpu.VMEM((CONV_W // LANES, tr + 2 * CONV_HALO, LANES), F32)],
        compiler_params=_cparams(1),
        name="conformer_conv",
    )(*_conv_args(a, b_sig, cg, dw_w, dw_b, ln_g, ln_b, w_pw))


def _merge_kernel(a_ref, f_ref, c_ref, g0_ref, g1_ref, g2_ref, wa_ref, wf_ref, wc_ref, o_ref):
    y_attn = _dot(a_ref[0], wa_ref[0:GROUP_W, :])
    for g in range(1, N_KV_HEADS):
        y_attn = y_attn + _dot(a_ref[g], wa_ref[g * GROUP_W:(g + 1) * GROUP_W, :])
    merged = (g0_ref[...].astype(F32) * y_attn
              + g1_ref[...].astype(F32) * _dot(f_ref[...], wf_ref[...])
              + g2_ref[...].astype(F32) * _dot(c_ref[...], wc_ref[...]))
    o_ref[...] = merged.astype(o_ref.dtype)


def _merge(attn, four, conv, mg, w_attn_up, w_four_up, w_conv_up, layer, tm):
    m = four.shape[0]
    tn = 1024
    nj = D_MODEL // tn
    return pl.pallas_call(
        _merge_kernel,
        grid=(nj, m // tm),
        in_specs=[pl.BlockSpec((N_KV_HEADS, tm, GROUP_W), lambda j, i: (0, i, 0)),
                  pl.BlockSpec((tm, FOURIER_W), lambda j, i: (i, 0)),
                  pl.BlockSpec((tm, CONV_W), lambda j, i: (i, 0)),
                  pl.BlockSpec((tm, tn), lambda j, i: (i, j)),
                  pl.BlockSpec((tm, tn), lambda j, i: (i, nj + j)),
                  pl.BlockSpec((tm, tn), lambda j, i: (i, 2 * nj + j)),
                  pl.BlockSpec((None, ATTN_W, tn), lambda j, i: (layer, 0, j)),
                  pl.BlockSpec((None, FOURIER_W, tn), lambda j, i: (layer, 0, j)),
                  pl.BlockSpec((None, CONV_W, tn), lambda j, i: (layer, 0, j))],
        out_specs=pl.BlockSpec((tm, tn), lambda j, i: (i, j)),
        out_shape=jax.ShapeDtypeStruct((m, D_MODEL), BF16),
        compiler_params=_cparams(2),
        name="branch_merge",
    )(attn, four, conv, mg, mg, mg, w_attn_up, w_four_up, w_conv_up)


def _out_kernel(a_ref, w_ref, x_ref, gt_ref, o_ref):
    o_ref[...] = x_ref[...] + gt_ref[...] * _dot(a_ref[...], w_ref[...])


def _out_proj(merged, w_out, x2d, mods, layer, row_of_tile, tm):
    m = merged.shape[0]
    tn = 1024
    return pl.pallas_call(
        _out_kernel,
        grid=(D_MODEL // tn, m // tm),
        in_specs=[pl.BlockSpec((tm, D_MODEL), lambda j, i: (i, 0)),
                  pl.BlockSpec((None, D_MODEL, tn), lambda j, i: (layer, 0, j)),
                  pl.BlockSpec((tm, tn), lambda j, i: (i, j)),
                  pl.BlockSpec((None, None, 1, tn), lambda j, i: (layer, row_of_tile(i), 0, 2 * (D_MODEL // tn) + j))],
        out_specs=pl.BlockSpec((tm, tn), lambda j, i: (i, j)),
        out_shape=jax.ShapeDtypeStruct((m, D_MODEL), F32),
        compiler_params=_cparams(2),
        name="out_proj",
    )(merged, w_out, x2d, mods)


def _branches_and_merge(h, w, layer, tm, gates_and_attention, fourier, conv_seq, conv_tr):
    w_in = w["w_in"]
    ca = _proj(h, w_in, layer, CA_OFF, CONV_W, "none", tm)
    cb = _proj(h, w_in, layer, CB_OFF, CONV_W, "sigmoid", tm)
    cg = _proj(h, w_in, layer, CG_OFF, CONV_W, "silu", tm)
    conv_in = (ca, cb, cg, w["conv_dw_w"], w["conv_dw_b"], w["conv_ln_g"], w["conv_ln_b"], w["w_conv_pw"])
    q = _proj(h, w_in, layer, Q_OFF, ATTN_W, "none", tm)
    conv = _conformer(*conv_in, layer, conv_seq, conv_tr)
    kv = _proj(h, w_in, layer, K_OFF, 2 * KV_W, "none", tm)
    ag = _proj(h, w_in, layer, AG_OFF, ATTN_W, "silu", tm)
    fu = _proj(h, w_in, layer, F_OFF, FOURIER_W, "none", tm)
    fg = _proj(h, w_in, layer, FG_OFF, FOURIER_W, "silu", tm)
    mg, attn = gates_and_attention(h, q, kv, ag)
    four = fourier(fu, fg)
    merged = _merge(attn, four, conv, mg, w["w_attn_up"], w["w_fourier_up"], w["w_conv_up"], layer, min(tm, 512))
    return merged, kv


@jax.jit
def _block(x, c, ctx, c_ctx, norm_g, w_mod, b_mod, w_in, q_norm_g, k_norm_g, attn_sink, w_attn_up,
           w_fourier_mix, w_fourier_up, conv_dw_w, conv_dw_b, conv_ln_g, conv_ln_b, w_conv_pw, w_conv_up, w_out):
    w = dict(w_in=w_in[:, :, :MG_OFF].astype(BF16), w_mg=w_in[:, :, MG_OFF:].astype(BF16),
             w_attn_up=w_attn_up.astype(BF16), w_fourier_up=w_fourier_up.astype(BF16),
             w_conv_pw=w_conv_pw.astype(BF16), w_conv_up=w_conv_up.astype(BF16),
             conv_dw_w=conv_dw_w, conv_dw_b=conv_dw_b, conv_ln_g=conv_ln_g, conv_ln_b=conv_ln_b)
    w_out_b = w_out.astype(BF16)
    w_mix_b = w_fourier_mix.astype(BF16)

    rope_tab = _rope_tables()
    cc_t, sc_t = _channel_dft_tables()
    lat_tabs = (_stage1_table(), _stage2_table(), cc_t, sc_t, _interleave_table())
    cls_t = _ctx_dft_table()

    cond = jnp.concatenate([c, c_ctx[None, :], jnp.zeros((8 - BATCH - 1, D_MODEL), F32)], axis=0)
    mods = _modulation(cond, w_mod, b_mod).reshape(DEPTH, 8, 1, 3 * D_MODEL)
    ctx_row = BATCH

    x2d = x.reshape(BATCH * SEQ, D_MODEL)
    ctx2d = ctx.reshape(BATCH * CTX_LEN, D_MODEL)
    lat_tm = 1024
    ctx_tm = BATCH * CTX_LEN
    norm_tm = 256
    lat_tiles_per_batch = SEQ // norm_tm

    for l in range(DEPTH):
        sink = attn_sink[l]
        qg = q_norm_g[l].reshape(1, HEAD_DIM)
        kg = k_norm_g[l].reshape(1, HEAD_DIM)

        h_ctx = _adaln(ctx2d, norm_g, mods, l, lambda i: ctx_row, norm_tm)
        if l < DEPTH - 1:
            merged_c, kv_c = _branches_and_merge(
                h_ctx, w, l, ctx_tm,
                gates_and_attention=lambda h_, q, kv, ag: (
                    _proj(h_, w["w_mg"], l, 0, N_BRANCH * D_MODEL, "sigmoid", ctx_tm),
                    _context_attention(q, kv, ag, sink, qg, kg)),
                fourier=lambda fu, fg: _context_fourier(fu, fg, w_mix_b, l, cls_t, cc_t, sc_t),
                conv_seq=CTX_LEN, conv_tr=CTX_LEN)
            ctx_next = _out_proj(merged_c, w_out_b, ctx2d, mods, l, lambda i: ctx_row, ctx_tm)
        else:
            kv_c = _proj(h_ctx, w["w_in"], l, K_OFF, 2 * KV_W, "none", ctx_tm)
            ctx_next = ctx2d

        h = _adaln(x2d, norm_g, mods, l, lambda i: i // lat_tiles_per_batch, norm_tm)
        merged, _ = _branches_and_merge(
            h, w, l, lat_tm,
            gates_and_attention=lambda h_, q, kv, ag: _gates_and_attention(
                h_, w["w_mg"], l, q, kv, ag, kv_c, sink, qg, kg, rope_tab),
            fourier=lambda fu, fg: _latent_fourier(fu, fg, w_mix_b, l, lat_tabs),
            conv_seq=SEQ, conv_tr=256)
        out_tm = 512
        x2d = _out_proj(merged, w_out_b, x2d, mods, l, lambda i: i // (SEQ // out_tm), out_tm)
        ctx2d = ctx_next
    return x2d.reshape(BATCH, SEQ, D_MODEL)


def kernel(x, c, ctx, c_ctx, norm_g, w_mod, b_mod, w_in, q_norm_g, k_norm_g, attn_sink, w_attn_up, w_fourier_mix,
           w_fourier_up, conv_dw_w, conv_dw_b, conv_ln_g, conv_ln_b, w_conv_pw, w_conv_up, w_out):
    return _block(x, c, ctx, c_ctx, norm_g, w_mod, b_mod, w_in, q_norm_g, k_norm_g, attn_sink, w_attn_up,
                  w_fourier_mix, w_fourier_up, conv_dw_w, conv_dw_b, conv_ln_g, conv_ln_b, w_conv_pw, w_conv_up,
                  w_out)
```

```python
import functools

import numpy as np
import jax
import jax.numpy as jnp
from jax import lax
from jax.experimental import pallas as pl
from jax.experimental.pallas import tpu as pltpu

D_MODEL = 4096
BATCH = 2
SEQ = 8192
DEPTH = 2
CTX_LEN = 256
GRID_W = 64

N_HEADS = 16
N_KV_HEADS = 4
HEAD_DIM = 128
GROUP = N_HEADS // N_KV_HEADS
ATTN_W = N_HEADS * HEAD_DIM
KV_W = N_KV_HEADS * HEAD_DIM
WINDOW = 128
BLOCK = 128
ROPE_BASE = 10000.0

FOURIER_GROUPS = 4
FOURIER_GROUP_W = 256
FOURIER_W = FOURIER_GROUPS * FOURIER_GROUP_W

CONV_W = 1024
CONV_K = 31
CONV_HALO = 16

N_BRANCH = 3
EPS = 1e-6
NEG_INF = -1e30

Q_OFF = 0
K_OFF = Q_OFF + ATTN_W
AG_OFF = K_OFF + 2 * KV_W
F_OFF = AG_OFF + ATTN_W
FG_OFF = F_OFF + FOURIER_W
CA_OFF = FG_OFF + FOURIER_W
CB_OFF = CA_OFF + CONV_W
CG_OFF = CB_OFF + CONV_W
MG_OFF = CG_OFF + CONV_W
IN_W = MG_OFF + N_BRANCH * D_MODEL

FFT_N1 = 128
FFT_N2 = 64
FFT_KB = 16

V7X_VMEM_LIMIT = 56 * 1024 * 1024
LANES = 128

BF16 = jnp.bfloat16
F32 = jnp.float32


def _cparams(n_axes, vmem_limit_bytes=V7X_VMEM_LIMIT):
    return pltpu.CompilerParams(dimension_semantics=("arbitrary",) * n_axes,
                                vmem_limit_bytes=vmem_limit_bytes)


def _silu(v):
    return v * jax.nn.sigmoid(v)


def _act(v, act):
    if act == "silu":
        return _silu(v)
    if act == "sigmoid":
        return jax.nn.sigmoid(v)
    return v


def _dot(a, b):
    return jnp.dot(a, b, preferred_element_type=F32)


def _dot_nt(a, b):
    return lax.dot_general(a, b, (((1,), (1,)), ((), ())), preferred_element_type=F32)


def _cast_kernel(w_ref, o_ref):
    o_ref[...] = w_ref[...].astype(o_ref.dtype)


def _cast_cols(w, col_off, n_cols):
    depth, k, _ = w.shape
    tk, tn = 2048, 1024
    return pl.pallas_call(
        _cast_kernel,
        grid=(depth, k // tk, n_cols // tn),
        in_specs=[pl.BlockSpec((None, tk, tn), lambda l, r, j: (l, r, col_off // tn + j))],
        out_specs=pl.BlockSpec((None, tk, tn), lambda l, r, j: (l, r, j)),
        out_shape=jax.ShapeDtypeStruct((depth, k, n_cols), BF16),
        compiler_params=_cparams(3),
        name="cast_cols",
    )(w)


def _mod_kernel(c_ref, w_ref, b_ref, o_ref):
    s = _silu(c_ref[...]).astype(BF16)
    o_ref[...] = _dot(s, w_ref[...].astype(BF16)) + b_ref[...]


def _modulation(cc, w_mod, b_mod):
    tn = 512
    return pl.pallas_call(
        _mod_kernel,
        grid=(DEPTH, 3 * D_MODEL // tn),
        in_specs=[pl.BlockSpec((8, D_MODEL), lambda l, j: (0, 0)),
                  pl.BlockSpec((None, D_MODEL, tn), lambda l, j: (l, 0, j)),
                  pl.BlockSpec((None, 1, tn), lambda l, j: (l, 0, j))],
        out_specs=pl.BlockSpec((None, 8, tn), lambda l, j: (l, 0, j)),
        out_shape=jax.ShapeDtypeStruct((DEPTH, 8, 3 * D_MODEL), F32),
        compiler_params=_cparams(2),
        name="modulation",
    )(cc, w_mod, b_mod.reshape(DEPTH, 1, 3 * D_MODEL))


def _adaln_kernel(x_ref, g_ref, sh_ref, sc_ref, o_ref):
    x = x_ref[...]
    ms = jnp.mean(x * x, axis=-1, keepdims=True)
    y = x * lax.rsqrt(ms + EPS) * g_ref[...]
    o_ref[...] = (y * (1.0 + sc_ref[...]) + sh_ref[...]).astype(o_ref.dtype)


def _adaln(x2d, norm_g, mods, layer, row_of_tile, tm):
    m = x2d.shape[0]
    return pl.pallas_call(
        _adaln_kernel,
        grid=(m // tm,),
        in_specs=[pl.BlockSpec((tm, D_MODEL), lambda i: (i, 0)),
                  pl.BlockSpec((None, 1, D_MODEL), lambda i: (layer, 0, 0)),
                  pl.BlockSpec((None, None, 1, D_MODEL), lambda i: (layer, row_of_tile(i), 0, 0)),
                  pl.BlockSpec((None, None, 1, D_MODEL), lambda i: (layer, row_of_tile(i), 0, 1))],
        out_specs=pl.BlockSpec((tm, D_MODEL), lambda i: (i, 0)),
        out_shape=jax.ShapeDtypeStruct((m, D_MODEL), BF16),
        compiler_params=_cparams(1),
        name="adaln",
    )(x2d, norm_g.reshape(DEPTH, 1, D_MODEL), mods, mods)


def _proj_kernel(a_ref, w_ref, o_ref, *, act):
    o_ref[...] = _act(_dot(a_ref[...], w_ref[...]), act).astype(o_ref.dtype)


def _proj(h, w_in, layer, col_off, n_cols, act, tm):
    m, k = h.shape
    tn = 1024
    return pl.pallas_call(
        functools.partial(_proj_kernel, act=act),
        grid=(n_cols // tn, m // tm),
        in_specs=[pl.BlockSpec((tm, k), lambda j, i: (i, 0)),
                  pl.BlockSpec((None, k, tn), lambda j, i: (layer, 0, col_off // tn + j))],
        out_specs=pl.BlockSpec((tm, tn), lambda j, i: (i, j)),
        out_shape=jax.ShapeDtypeStruct((m, n_cols), BF16),
        compiler_params=_cparams(2),
        name="in_proj",
    )(h, w_in)


def _rms_head(x, g):
    ms = jnp.mean(x * x, axis=-1, keepdims=True)
    return x * lax.rsqrt(ms + EPS) * g


def _rope(y, cos, sin_signed):
    lane = lax.broadcasted_iota(jnp.int32, y.shape, 1)
    partner = jnp.where((lane & 32) == 0, pltpu.roll(y, 96, 1), pltpu.roll(y, 32, 1))
    return y * cos + partner * sin_signed


def _rope_tables():
    rows = SEQ // GRID_W
    row = np.repeat(np.arange(rows, dtype=np.float64), GRID_W)
    col = np.tile(np.arange(GRID_W, dtype=np.float64), rows)
    axis_dim = HEAD_DIM // 2
    inv_freq = ROPE_BASE ** (-np.arange(0, axis_dim, 2, dtype=np.float64) / axis_dim)
    ang_r = row[:, None] * inv_freq[None, :]
    ang_c = col[:, None] * inv_freq[None, :]
    ang = np.concatenate([ang_r, ang_r, ang_c, ang_c], axis=-1)
    sign = np.where((np.arange(HEAD_DIM) & 32) == 0, -1.0, 1.0)
    return jnp.asarray(np.concatenate([np.cos(ang), np.sin(ang) * sign[None, :]], axis=1), F32)


ATTN_SCALE = HEAD_DIM ** -0.5
GROUP_W = GROUP * HEAD_DIM
N_BLOCKS = SEQ // BLOCK
MG_TM = 1024
MG_TN = 768
MG_PARTS = 4
UNITS_PER_STEP = 2
assert (MG_TM // BLOCK) * N_KV_HEADS == UNITS_PER_STEP * (N_BRANCH * D_MODEL // MG_TN)


def _attn_scores(q_ref, kp, kc, kn, kx, tp, tc, tn, qg, kg, first, last):
    def kprep(k, t):
        return _rope(_rms_head(k.astype(F32), kg), t[:, :HEAD_DIM], t[:, HEAD_DIM:]).astype(BF16)

    kp, kc, kn = kprep(kp, tp), kprep(kc, tc), kprep(kn, tn)
    kx = _rms_head(kx.astype(F32), kg).astype(BF16)
    q = jnp.concatenate(
        [(_rope(_rms_head(q_ref[:, j * HEAD_DIM:(j + 1) * HEAD_DIM].astype(F32), qg), tc[:, :HEAD_DIM], tc[:, HEAD_DIM:])
          * ATTN_SCALE).astype(BF16) for j in range(GROUP)], axis=0)
    rows = GROUP * BLOCK
    qi = lax.broadcasted_iota(jnp.int32, (rows, BLOCK), 0) & (BLOCK - 1)
    col = lax.broadcasted_iota(jnp.int32, (rows, BLOCK), 1)
    s_p = jnp.where(col >= qi + jnp.where(first, BLOCK, 0), _dot_nt(q, kp), NEG_INF)
    s_c = _dot_nt(q, kc)
    s_n = jnp.where(col <= qi - jnp.where(last, BLOCK, 0), _dot_nt(q, kn), NEG_INF)
    s_x = _dot_nt(q, kx)
    return s_p, s_c, s_n, s_x


def _attn_finish(scores, vp, vc, vn, vx, ag_ref, sinks, o_ref):
    s_p, s_c, s_n, s_x = scores
    rows = GROUP * BLOCK
    rgrp = lax.broadcasted_iota(jnp.int32, (rows, 1), 0) >> 7
    sink = jnp.full((rows, 1), sinks[0], F32)
    for j in range(1, GROUP):
        sink = jnp.where(rgrp == j, sinks[j], sink)
    mx = jnp.maximum(jnp.maximum(s_p, s_c), s_n)
    for c in range(CTX_LEN // BLOCK):
        mx = jnp.maximum(mx, s_x[:, c * BLOCK:(c + 1) * BLOCK])
    m = jnp.maximum(jnp.max(mx, axis=-1, keepdims=True), sink)

    def pv(s, v):
        p = jnp.exp((s - m).astype(BF16))
        return _dot(p, jnp.concatenate([v, jnp.ones_like(v)], axis=1))

    acc = pv(s_p, vp) + pv(s_c, vc) + pv(s_n, vn) + pv(s_x, vx)
    den = acc[:, HEAD_DIM:HEAD_DIM + 1] + jnp.exp(sink - m)
    o = acc[:, :HEAD_DIM] * (1.0 / den)
    for j in range(GROUP):
        cs = slice(j * HEAD_DIM, (j + 1) * HEAD_DIM)
        o_ref[:, cs] = (o[j * BLOCK:(j + 1) * BLOCK] * ag_ref[:, cs].astype(F32)).astype(o_ref.dtype)


def _gate_attn_kernel(sink_ref, h_ref, w_ref, qg_ref, kg_ref, q_ref, ag_ref, kp_ref, kc_ref, kn_ref,
                      vp_ref, vc_ref, vn_ref, kx_ref, vx_ref, tp_ref, tc_ref, tn_ref, mg_ref, o_ref):
    i = pl.program_id(0)
    j = pl.program_id(1)
    steps_per_block = N_KV_HEADS // UNITS_PER_STEP
    blk = lax.rem(i, SEQ // MG_TM) * (MG_TM // BLOCK) + j // steps_per_block
    first = blk == 0
    last = blk == N_BLOCKS - 1
    qg = qg_ref[...]
    kg = kg_ref[...]
    tp, tc, tn = tp_ref[...], tc_ref[...], tn_ref[...]

    def gate_rows(part):
        rs = slice(part * MG_TM // MG_PARTS, (part + 1) * MG_TM // MG_PARTS)
        mg_ref[rs, :] = jax.nn.sigmoid(_dot(h_ref[rs, :], w_ref[...])).astype(mg_ref.dtype)

    gate_rows(0)
    scores = []
    for u in range(UNITS_PER_STEP):
        hs = slice(u * HEAD_DIM, (u + 1) * HEAD_DIM)
        scores.append(_attn_scores(q_ref.at[:, u * GROUP_W:(u + 1) * GROUP_W], kp_ref[:, hs], kc_ref[:, hs],
                                   kn_ref[:, hs], kx_ref[:, hs], tp, tc, tn, qg, kg, first, last))
    gate_rows(1)
    for u in range(UNITS_PER_STEP):
        grp = lax.rem(j, steps_per_block) * UNITS_PER_STEP + u
        hs = slice(u * HEAD_DIM, (u + 1) * HEAD_DIM)
        sinks = [sink_ref[grp * GROUP + g] for g in range(GROUP)]
        _attn_finish(scores[u], vp_ref[:, hs], vc_ref[:, hs], vn_ref[:, hs], vx_ref[:, hs],
                     ag_ref.at[:, u * GROUP_W:(u + 1) * GROUP_W], sinks, o_ref.at[u])
    for part in range(2, MG_PARTS):
        gate_rows(part)


def _gates_and_attention(h, w_mg, layer, q, kv, ag, kv_ctx, sink, q_g, k_g, rope_tab):
    m = h.shape[0]
    upb = UNITS_PER_STEP
    pairs = N_KV_HEADS // upb
    blocks_per_tile = MG_TM // BLOCK
    tiles_per_seq = SEQ // MG_TM

    def blk(i, j):
        return lax.rem(i, tiles_per_seq) * blocks_per_tile + j // pairs

    def seq0(i):
        return (i // tiles_per_seq) * N_BLOCKS

    def cur(i, j):
        return seq0(i) + blk(i, j)

    def prev(i, j):
        return seq0(i) + jnp.maximum(blk(i, j) - 1, 0)

    def nxt(i, j):
        return seq0(i) + jnp.minimum(blk(i, j) + 1, N_BLOCKS - 1)

    def pair(j):
        return lax.rem(j, pairs)

    kw = upb * HEAD_DIM
    k_spec = lambda rowf: pl.BlockSpec((BLOCK, kw), lambda i, j: (rowf(i, j), pair(j)))
    v_spec = lambda rowf: pl.BlockSpec((BLOCK, kw), lambda i, j: (rowf(i, j), pairs + pair(j)))
    t_spec = lambda rowf: pl.BlockSpec((BLOCK, 2 * HEAD_DIM), lambda i, j: (rowf(i, j) - seq0(i), 0))
    in_specs = [
        pl.BlockSpec(memory_space=pltpu.SMEM),
        pl.BlockSpec((MG_TM, D_MODEL), lambda i, j: (i, 0)),
        pl.BlockSpec((None, D_MODEL, MG_TN), lambda i, j: (layer, 0, j)),
        pl.BlockSpec((1, HEAD_DIM), lambda i, j: (0, 0)),
        pl.BlockSpec((1, HEAD_DIM), lambda i, j: (0, 0)),
        pl.BlockSpec((BLOCK, upb * GROUP_W), lambda i, j: (cur(i, j), pair(j))),
        pl.BlockSpec((BLOCK, upb * GROUP_W), lambda i, j: (cur(i, j), pair(j))),
        k_spec(prev), k_spec(cur), k_spec(nxt),
        v_spec(prev), v_spec(cur), v_spec(nxt),
        pl.BlockSpec((CTX_LEN, kw), lambda i, j: (i // tiles_per_seq, pair(j))),
        pl.BlockSpec((CTX_LEN, kw), lambda i, j: (i // tiles_per_seq, pairs + pair(j))),
        t_spec(prev), t_spec(cur), t_spec(nxt),
    ]
    out_specs = [pl.BlockSpec((MG_TM, MG_TN), lambda i, j: (i, j)),
                 pl.BlockSpec((upb, BLOCK, GROUP_W), lambda i, j: (pair(j), cur(i, j), 0))]
    return pl.pallas_call(
        _gate_attn_kernel,
        grid=(m // MG_TM, N_BRANCH * D_MODEL // MG_TN),
        in_specs=in_specs,
        out_specs=out_specs,
        out_shape=[jax.ShapeDtypeStruct((m, N_BRANCH * D_MODEL), BF16),
                   jax.ShapeDtypeStruct((N_KV_HEADS, m, GROUP_W), BF16)],
        compiler_params=_cparams(2),
        name="gates_and_attention",
    )(sink, h, w_mg, q_g, k_g, q, ag, kv, kv, kv, kv, kv, kv, kv_ctx, kv_ctx, rope_tab, rope_tab, rope_tab)


def _ctx_attn_kernel(sink_ref, q_ref, kv_ref, ag_ref, qg_ref, kg_ref, o_ref):
    qg = qg_ref[...]
    kg = kg_ref[...]
    for h in range(N_KV_HEADS):
        kx = _rms_head(kv_ref[:, h * HEAD_DIM:(h + 1) * HEAD_DIM].astype(F32), kg).astype(BF16)
        v = kv_ref[:, KV_W + h * HEAD_DIM:KV_W + (h + 1) * HEAD_DIM]
        for j in range(GROUP):
            hh = h * GROUP + j
            cs = slice(hh * HEAD_DIM, (hh + 1) * HEAD_DIM)
            qn = (_rms_head(q_ref[:, cs].astype(F32), qg) * ATTN_SCALE).astype(BF16)
            s = _dot_nt(qn, kx)
            sink = sink_ref[hh]
            m = jnp.maximum(jnp.max(s, axis=-1, keepdims=True), sink)
            p = jnp.exp(s - m)
            den = jnp.sum(p, axis=-1, keepdims=True) + jnp.exp(sink - m)
            o = _dot(p.astype(BF16), v) * (1.0 / den)
            o_ref[h, :, j * HEAD_DIM:(j + 1) * HEAD_DIM] = (o * ag_ref[:, cs].astype(F32)).astype(o_ref.dtype)


def _context_attention(q, kv, ag, sink, q_g, k_g):
    return pl.pallas_call(
        _ctx_attn_kernel,
        grid=(BATCH,),
        in_specs=[pl.BlockSpec(memory_space=pltpu.SMEM),
                  pl.BlockSpec((CTX_LEN, ATTN_W), lambda b: (b, 0)),
                  pl.BlockSpec((CTX_LEN, 2 * KV_W), lambda b: (b, 0)),
                  pl.BlockSpec((CTX_LEN, ATTN_W), lambda b: (b, 0)),
                  pl.BlockSpec((1, HEAD_DIM), lambda b: (0, 0)),
                  pl.BlockSpec((1, HEAD_DIM), lambda b: (0, 0))],
        out_specs=pl.BlockSpec((N_KV_HEADS, CTX_LEN, GROUP_W), lambda b: (0, b, 0)),
        out_shape=jax.ShapeDtypeStruct((N_KV_HEADS, BATCH * CTX_LEN, GROUP_W), BF16),
        compiler_params=_cparams(1),
        name="context_attention",
    )(sink, q, kv, ag, q_g, k_g)


def _table(values):
    return jnp.asarray(values, F32).astype(BF16)


def _channel_dft_tables():
    n = FOURIER_GROUP_W
    ang = 2.0 * np.pi * np.outer(np.arange(n), np.arange(n)) / n
    s = n ** -0.5
    return _table(np.cos(ang) * s), _table(np.sin(ang) * s)


def _stage1_table():
    ang = 2.0 * np.pi * np.outer(np.arange(FFT_N2), np.arange(FFT_N2)) / FFT_N2
    s = FFT_N2 ** -0.5
    return _table(np.concatenate([np.cos(ang), -np.sin(ang)], axis=0) * s)


def _stage2_table():
    k2 = np.arange(FFT_N2)[:, None, None]
    k1 = np.arange(FFT_N1)[None, :, None]
    l1 = np.arange(FFT_N1)[None, None, :]
    ang = 2.0 * np.pi * ((l1 * (k2 + FFT_N2 * k1)) % SEQ) / SEQ
    tr = np.cos(ang) * FFT_N1 ** -0.5
    ti = -np.sin(ang) * FFT_N1 ** -0.5
    top = np.concatenate([tr, -ti], axis=2)
    bot = np.concatenate([ti, tr], axis=2)
    return _table(np.concatenate([top, bot], axis=1))


def _interleave_table():
    n = 8 * FFT_KB
    p = np.zeros((n, n), np.float32)
    for b in range(8):
        for j in range(FFT_KB):
            p[b * FFT_KB + j, j * 8 + b] = 1.0
    return _table(p)


def _fft1_kernel(d_ref, u_ref, o_ref):
    o_ref[...] = _dot(d_ref[...], u_ref[...]).astype(o_ref.dtype)


def _fft2_kernel(z_ref, t_ref, cc_ref, sc_ref, w_ref, p_ref, fg_ref, o_ref, ybuf):
    cc = cc_ref[...]
    sc = sc_ref[...]
    w = w_ref[...]
    for j in range(FFT_KB):
        zst = jnp.concatenate([z_ref[0, j], z_ref[1, j]], axis=0)
        y = _dot(t_ref[j], zst)
        a = _dot(y[:FFT_N1].astype(BF16), cc) + _dot(y[FFT_N1:].astype(BF16), sc)
        ybuf[j * FFT_N1:(j + 1) * FFT_N1, :] = _dot(a.astype(BF16), w)
    perm = p_ref[...]
    for a8 in range(FFT_N1 // 8):
        blk = jnp.concatenate([ybuf[j * FFT_N1 + 8 * a8:j * FFT_N1 + 8 * a8 + 8, :] for j in range(FFT_KB)], axis=0)
        nat = _dot(perm, blk.astype(BF16))
        gate = fg_ref[8 * a8:8 * a8 + 8].reshape(8 * FFT_KB, FOURIER_GROUP_W).astype(F32)
        o_ref[8 * a8:8 * a8 + 8] = (nat * gate).astype(o_ref.dtype).reshape(8, FFT_KB, FOURIER_GROUP_W)


def _latent_fourier(u, fg, w_mix, layer, tabs):
    d1, t2, cc, sc, perm = tabs
    cols = FFT_N1 * FOURIER_W
    tn = 8192
    z = pl.pallas_call(
        _fft1_kernel,
        grid=(BATCH, cols // tn),
        in_specs=[pl.BlockSpec((2 * FFT_N2, FFT_N2), lambda b, j: (0, 0)),
                  pl.BlockSpec((None, FFT_N2, tn), lambda b, j: (b, 0, j))],
        out_specs=pl.BlockSpec((None, 2 * FFT_N2, tn), lambda b, j: (b, 0, j)),
        out_shape=jax.ShapeDtypeStruct((BATCH, 2 * FFT_N2, cols), BF16),
        compiler_params=_cparams(2),
        name="fourier_stage1",
    )(d1, u.reshape(BATCH, FFT_N2, cols))
    z = z.reshape(BATCH, 2, FFT_N2, FFT_N1, FOURIER_W)
    gw = FOURIER_GROUP_W
    out = pl.pallas_call(
        _fft2_kernel,
        grid=(BATCH, FFT_N2 // FFT_KB, FOURIER_GROUPS),
        in_specs=[pl.BlockSpec((None, 2, FFT_KB, FFT_N1, gw), lambda b, t, g: (b, 0, t, 0, g)),
                  pl.BlockSpec((FFT_KB, 2 * FFT_N1, 2 * FFT_N1), lambda b, t, g: (t, 0, 0)),
                  pl.BlockSpec((gw, gw), lambda b, t, g: (0, 0)),
                  pl.BlockSpec((gw, gw), lambda b, t, g: (0, 0)),
                  pl.BlockSpec((None, None, gw, gw), lambda b, t, g: (layer, g, 0, 0)),
                  pl.BlockSpec((8 * FFT_KB, 8 * FFT_KB), lambda b, t, g: (0, 0)),
                  pl.BlockSpec((None, FFT_N1, FFT_KB, gw), lambda b, t, g: (b, 0, t, g))],
        out_specs=pl.BlockSpec((None, FFT_N1, FFT_KB, gw), lambda b, t, g: (b, 0, t, g)),
        out_shape=jax.ShapeDtypeStruct((BATCH, FFT_N1, FFT_N2, FOURIER_W), BF16),
        scratch_shapes=[pltpu.VMEM((FFT_KB * FFT_N1, gw), F32)],
        compiler_params=_cparams(3),
        name="fourier_stage2",
    )(z, t2, cc, sc, w_mix, perm, fg.reshape(BATCH, FFT_N1, FFT_N2, FOURIER_W))
    return out.reshape(BATCH * SEQ, FOURIER_W)


def _ctx_dft_table():
    n = CTX_LEN
    ang = 2.0 * np.pi * np.outer(np.arange(n), np.arange(n)) / n
    s = n ** -0.5
    return _table(np.concatenate([np.cos(ang), -np.sin(ang)], axis=1) * s)


def _ctx_fourier_kernel(u_ref, fg_ref, cls_ref, cc_ref, sc_ref, w_ref, o_ref):
    u = u_ref[...]
    st = jnp.concatenate([_dot(u, cc_ref[...]), _dot(u, sc_ref[...])], axis=0).astype(BF16)
    f = _dot(cls_ref[...], st)
    y = _dot(f.astype(BF16), w_ref[...])
    o_ref[...] = (y * fg_ref[...].astype(F32)).astype(o_ref.dtype)


def _context_fourier(u, fg, w_mix, layer, cls, cc, sc):
    gw = FOURIER_GROUP_W
    return pl.pallas_call(
        _ctx_fourier_kernel,
        grid=(BATCH, FOURIER_GROUPS),
        in_specs=[pl.BlockSpec((CTX_LEN, gw), lambda b, g: (b, g)),
                  pl.BlockSpec((CTX_LEN, gw), lambda b, g: (b, g)),
                  pl.BlockSpec((CTX_LEN, 2 * CTX_LEN), lambda b, g: (0, 0)),
                  pl.BlockSpec((gw, gw), lambda b, g: (0, 0)),
                  pl.BlockSpec((gw, gw), lambda b, g: (0, 0)),
                  pl.BlockSpec((None, None, gw, gw), lambda b, g: (layer, g, 0, 0))],
        out_specs=pl.BlockSpec((CTX_LEN, gw), lambda b, g: (b, g)),
        out_shape=jax.ShapeDtypeStruct((BATCH * CTX_LEN, FOURIER_W), BF16),
        compiler_params=_cparams(2),
        name="context_fourier",
    )(u, fg, cls, cc, sc, w_mix)


def _conv_norm_act(u_ext, dww_ref, dwb_ref, lng_ref, lnb_ref, ubuf, tr):
    n_slab = CONV_W // LANES
    for c in range(n_slab):
        ubuf[c] = u_ext[:, c * LANES:(c + 1) * LANES]
    first = CONV_HALO - CONV_K // 2
    ys = []
    for c in range(n_slab):
        cs = slice(c * LANES, (c + 1) * LANES)
        acc = ubuf[c, first:first + tr, :] * dww_ref[0:1, cs]
        for k in range(1, CONV_K):
            acc = acc + ubuf[c, first + k:first + k + tr, :] * dww_ref[k:k + 1, cs]
        ys.append(acc)
    y = jnp.concatenate(ys, axis=1) + dwb_ref[...]
    mu = jnp.mean(y, axis=-1, keepdims=True)
    yc = y - mu
    var = jnp.mean(yc * yc, axis=-1, keepdims=True)
    return _silu(yc * lax.rsqrt(var + EPS) * lng_ref[...] + lnb_ref[...]).astype(BF16)


def _conv_kernel(a_ref, b_ref, ap_ref, bp_ref, an_ref, bn_ref, cg_ref, dww_ref, dwb_ref, lng_ref, lnb_ref,
                 wpw_ref, o_ref, ubuf, *, tiles_per_seq, tr):
    ti = lax.rem(pl.program_id(0), tiles_per_seq)
    u = a_ref[...].astype(F32) * b_ref[...].astype(F32)
    up = jnp.where(ti > 0, ap_ref[...].astype(F32) * bp_ref[...].astype(F32), 0.0)
    un = jnp.where(ti < tiles_per_seq - 1, an_ref[...].astype(F32) * bn_ref[...].astype(F32), 0.0)
    z = _conv_norm_act(jnp.concatenate([up, u, un], axis=0), dww_ref, dwb_ref, lng_ref, lnb_ref, ubuf, tr)
    o_ref[...] = (_dot(z, wpw_ref[...]) * cg_ref[...].astype(F32)).astype(o_ref.dtype)


def _conv_specs(m, tr, layer, tile_of):
    hb = tr // CONV_HALO
    n_hb = m // CONV_HALO
    tile = pl.BlockSpec((tr, CONV_W), lambda *idx: (tile_of(*idx), 0))
    halo_p = pl.BlockSpec((CONV_HALO, CONV_W), lambda *idx: (jnp.maximum(tile_of(*idx) * hb - 1, 0), 0))
    halo_n = pl.BlockSpec((CONV_HALO, CONV_W), lambda *idx: (jnp.minimum(tile_of(*idx) * hb + hb, n_hb - 1), 0))
    vec = pl.BlockSpec((None, 1, CONV_W), lambda *idx: (layer, 0, 0))
    specs = [tile, tile, halo_p, halo_p, halo_n, halo_n, tile,
             pl.BlockSpec((None, CONV_K, CONV_W), lambda *idx: (layer, 0, 0)),
             vec, vec, vec,
             pl.BlockSpec((None, CONV_W, CONV_W), lambda *idx: (layer, 0, 0))]
    return specs, tile


def _conv_args(a, b_sig, cg, dw_w, dw_b, ln_g, ln_b, w_pw):
    return (a, b_sig, a, b_sig, a, b_sig, cg, dw_w,
            dw_b.reshape(DEPTH, 1, CONV_W), ln_g.reshape(DEPTH, 1, CONV_W), ln_b.reshape(DEPTH, 1, CONV_W), w_pw)


def _conformer(a, b_sig, cg, dw_w, dw_b, ln_g, ln_b, w_pw, layer, seq_len, tr):
    m = a.shape[0]
    specs, tile = _conv_specs(m, tr, layer, lambda i: i)
    return pl.pallas_call(
        functools.partial(_conv_kernel, tiles_per_seq=seq_len // tr, tr=tr),
        grid=(m // tr,),
        in_specs=specs,
        out_specs=tile,
        out_shape=jax.ShapeDtypeStruct((m, CONV_W), BF16),
        scratch_shapes=[pltpu.VMEM((CONV_W // LANES, tr + 2 * CONV_HALO, LANES), F32)],
        compiler_params=_cparams(1),
        name="conformer_conv",
    )(*_conv_args(a, b_sig, cg, dw_w, dw_b, ln_g, ln_b, w_pw))


def _merge_kernel(a_ref, f_ref, c_ref, g0_ref, g1_ref, g2_ref, wa_ref, wf_ref, wc_ref, o_ref):
    y_attn = _dot(a_ref[0], wa_ref[0:GROUP_W, :])
    for g in range(1, N_KV_HEADS):
        y_attn = y_attn + _dot(a_ref[g], wa_ref[g * GROUP_W:(g + 1) * GROUP_W, :])
    merged = (g0_ref[...].astype(F32) * y_attn
              + g1_ref[...].astype(F32) * _dot(f_ref[...], wf_ref[...])
              + g2_ref[...].astype(F32) * _dot(c_ref[...], wc_ref[...]))
    o_ref[...] = merged.astype(o_ref.dtype)


def _merge(attn, four, conv, mg, w_attn_up, w_four_up, w_conv_up, layer, tm):
    m = four.shape[0]
    tn = 1024
    nj = D_MODEL // tn
    return pl.pallas_call(
        _merge_kernel,
        grid=(nj, m // tm),
        in_specs=[pl.BlockSpec((N_KV_HEADS, tm, GROUP_W), lambda j, i: (0, i, 0)),
                  pl.BlockSpec((tm, FOURIER_W), lambda j, i: (i, 0)),
                  pl.BlockSpec((tm, CONV_W), lambda j, i: (i, 0)),
                  pl.BlockSpec((tm, tn), lambda j, i: (i, j)),
                  pl.BlockSpec((tm, tn), lambda j, i: (i, nj + j)),
                  pl.BlockSpec((tm, tn), lambda j, i: (i, 2 * nj + j)),
                  pl.BlockSpec((None, ATTN_W, tn), lambda j, i: (layer, 0, j)),
                  pl.BlockSpec((None, FOURIER_W, tn), lambda j, i: (layer, 0, j)),
                  pl.BlockSpec((None, CONV_W, tn), lambda j, i: (layer, 0, j))],
        out_specs=pl.BlockSpec((tm, tn), lambda j, i: (i, j)),
        out_shape=jax.ShapeDtypeStruct((m, D_MODEL), BF16),
        compiler_params=_cparams(2),
        name="branch_merge",
    )(attn, four, conv, mg, mg, mg, w_attn_up, w_four_up, w_conv_up)


def _out_kernel(a_ref, w_ref, x_ref, gt_ref, o_ref):
    o_ref[...] = x_ref[...] + gt_ref[...] * _dot(a_ref[...], w_ref[...])


def _out_proj(merged, w_out, x2d, mods, layer, row_of_tile, tm):
    m = merged.shape[0]
    tn = 1024
    return pl.pallas_call(
        _out_kernel,
        grid=(D_MODEL // tn, m // tm),
        in_specs=[pl.BlockSpec((tm, D_MODEL), lambda j, i: (i, 0)),
                  pl.BlockSpec((None, D_MODEL, tn), lambda j, i: (layer, 0, j)),
                  pl.BlockSpec((tm, tn), lambda j, i: (i, j)),
                  pl.BlockSpec((None, None, 1, tn), lambda j, i: (layer, row_of_tile(i), 0, 2 * (D_MODEL // tn) + j))],
        out_specs=pl.BlockSpec((tm, tn), lambda j, i: (i, j)),
        out_shape=jax.ShapeDtypeStruct((m, D_MODEL), F32),
        compiler_params=_cparams(2),
        name="out_proj",
    )(merged, w_out, x2d, mods)


def _branches_and_merge(h, w, layer, tm, gates_and_attention, fourier, conv_seq, conv_tr):
    w_in = w["w_in"]
    conv_w = (w["conv_dw_w"], w["conv_dw_b"], w["conv_ln_g"], w["conv_ln_b"], w["w_conv_pw"])
    ca = _proj(h, w_in, layer, CA_OFF, CONV_W, "none", tm)
    cb = _proj(h, w_in, layer, CB_OFF, CONV_W, "sigmoid", tm)
    cg = _proj(h, w_in, layer, CG_OFF, CONV_W, "silu", tm)
    conv = _conformer(ca, cb, cg, *conv_w, layer, conv_seq, conv_tr)
    q = _proj(h, w_in, layer, Q_OFF, ATTN_W, "none", tm)
    kv = _proj(h, w_in, layer, K_OFF, 2 * KV_W, "none", tm)
    ag = _proj(h, w_in, layer, AG_OFF, ATTN_W, "silu", tm)
    fu = _proj(h, w_in, layer, F_OFF, FOURIER_W, "none", tm)
    fg = _proj(h, w_in, layer, FG_OFF, FOURIER_W, "silu", tm)
    mg, attn = gates_and_attention(h, q, kv, ag)
    four = fourier(fu, fg)
    merged = _merge(attn, four, conv, mg, w["w_attn_up"], w["w_fourier_up"], w["w_conv_up"], layer, min(tm, 512))
    return merged, kv


@jax.jit
def _block(x, c, ctx, c_ctx, norm_g, w_mod, b_mod, w_in, q_norm_g, k_norm_g, attn_sink, w_attn_up,
           w_fourier_mix, w_fourier_up, conv_dw_w, conv_dw_b, conv_ln_g, conv_ln_b, w_conv_pw, w_conv_up, w_out):
    w = dict(w_in=_cast_cols(w_in, 0, MG_OFF), w_mg=_cast_cols(w_in, MG_OFF, N_BRANCH * D_MODEL),
             w_attn_up=w_attn_up.astype(BF16), w_fourier_up=w_fourier_up.astype(BF16),
             w_conv_pw=w_conv_pw.astype(BF16), w_conv_up=w_conv_up.astype(BF16),
             conv_dw_w=conv_dw_w, conv_dw_b=conv_dw_b, conv_ln_g=conv_ln_g, conv_ln_b=conv_ln_b)
    w_out_b = w_out.astype(BF16)
    w_mix_b = w_fourier_mix.astype(BF16)

    rope_tab = _rope_tables()
    cc_t, sc_t = _channel_dft_tables()
    lat_tabs = (_stage1_table(), _stage2_table(), cc_t, sc_t, _interleave_table())
    cls_t = _ctx_dft_table()

    cond = jnp.concatenate([c, c_ctx[None, :], jnp.zeros((8 - BATCH - 1, D_MODEL), F32)], axis=0)
    mods = _modulation(cond, w_mod, b_mod).reshape(DEPTH, 8, 1, 3 * D_MODEL)
    ctx_row = BATCH

    x2d = x.reshape(BATCH * SEQ, D_MODEL)
    ctx2d = ctx.reshape(BATCH * CTX_LEN, D_MODEL)
    lat_tm = 1024
    ctx_tm = BATCH * CTX_LEN
    norm_tm = 512
    lat_tiles_per_batch = SEQ // norm_tm

    for l in range(DEPTH):
        sink = attn_sink[l]
        qg = q_norm_g[l].reshape(1, HEAD_DIM)
        kg = k_norm_g[l].reshape(1, HEAD_DIM)

        h_ctx = _adaln(ctx2d, norm_g, mods, l, lambda i: ctx_row, norm_tm)
        if l < DEPTH - 1:
            merged_c, kv_c = _branches_and_merge(
                h_ctx, w, l, ctx_tm,
                gates_and_attention=lambda h_, q, kv, ag: (
                    _proj(h_, w["w_mg"], l, 0, N_BRANCH * D_MODEL, "sigmoid", ctx_tm),
                    _context_attention(q, kv, ag, sink, qg, kg)),
                fourier=lambda fu, fg: _context_fourier(fu, fg, w_mix_b, l, cls_t, cc_t, sc_t),
                conv_seq=CTX_LEN, conv_tr=CTX_LEN)
            ctx_next = _out_proj(merged_c, w_out_b, ctx2d, mods, l, lambda i: ctx_row, ctx_tm)
        else:
            kv_c = _proj(h_ctx, w["w_in"], l, K_OFF, 2 * KV_W, "none", ctx_tm)
            ctx_next = ctx2d

        h = _adaln(x2d, norm_g, mods, l, lambda i: i // lat_tiles_per_batch, norm_tm)
        merged, _ = _branches_and_merge(
            h, w, l, lat_tm,
            gates_and_attention=lambda h_, q, kv, ag: _gates_and_attention(
                h_, w["w_mg"], l, q, kv, ag, kv_c, sink, qg, kg, rope_tab),
            fourier=lambda fu, fg: _latent_fourier(fu, fg, w_mix_b, l, lat_tabs),
            conv_seq=SEQ, conv_tr=256)
        out_tm = 512
        x2d = _out_proj(merged, w_out_b, x2d, mods, l, lambda i: i // (SEQ // out_tm), out_tm)
        ctx2d = ctx_next
    return x2d.reshape(BATCH, SEQ, D_MODEL)


def kernel(x, c, ctx, c_ctx, norm_g, w_mod, b_mod, w_in, q_norm_g, k_norm_g, attn_sink, w_attn_up, w_fourier_mix,
           w_fourier_up, conv_dw_w, conv_dw_b, conv_ln_g, conv_ln_b, w_conv_pw, w_conv_up, w_out):
    return _block(x, c, ctx, c_ctx, norm_g, w_mod, b_mod, w_in, q_norm_g, k_norm_g, attn_sink, w_attn_up,
                  w_fourier_mix, w_fourier_up, conv_dw_w, conv_dw_b, conv_ln_g, conv_ln_b, w_conv_pw, w_conv_up,
                  w_out)
```

```python
import functools

import numpy as np
import jax
import jax.numpy as jnp
from jax import lax
from jax.experimental import pallas as pl
from jax.experimental.pallas import tpu as pltpu

D_MODEL = 4096
BATCH = 2
SEQ = 8192
DEPTH = 2
CTX_LEN = 256
GRID_W = 64

N_HEADS = 16
N_KV_HEADS = 4
HEAD_DIM = 128
GROUP = N_HEADS // N_KV_HEADS
ATTN_W = N_HEADS * HEAD_DIM
KV_W = N_KV_HEADS * HEAD_DIM
WINDOW = 128
BLOCK = 128
ROPE_BASE = 10000.0

FOURIER_GROUPS = 4
FOURIER_GROUP_W = 256
FOURIER_W = FOURIER_GROUPS * FOURIER_GROUP_W

CONV_W = 1024
CONV_K = 31
CONV_HALO = 16

N_BRANCH = 3
EPS = 1e-6
NEG_INF = -1e30

Q_OFF = 0
K_OFF = Q_OFF + ATTN_W
AG_OFF = K_OFF + 2 * KV_W
F_OFF = AG_OFF + ATTN_W
FG_OFF = F_OFF + FOURIER_W
CA_OFF = FG_OFF + FOURIER_W
CB_OFF = CA_OFF + CONV_W
CG_OFF = CB_OFF + CONV_W
MG_OFF = CG_OFF + CONV_W
IN_W = MG_OFF + N_BRANCH * D_MODEL

FFT_N1 = 128
FFT_N2 = 64
FFT_KB = 16

V7X_VMEM_LIMIT = 56 * 1024 * 1024
LANES = 128

BF16 = jnp.bfloat16
F32 = jnp.float32


def _cparams(n_axes, vmem_limit_bytes=V7X_VMEM_LIMIT):
    return pltpu.CompilerParams(dimension_semantics=("arbitrary",) * n_axes,
                                vmem_limit_bytes=vmem_limit_bytes)


def _silu(v):
    return v * jax.nn.sigmoid(v)


def _act(v, act):
    if act == "silu":
        return _silu(v)
    if act == "sigmoid":
        return jax.nn.sigmoid(v)
    return v


def _dot(a, b):
    return jnp.dot(a, b, preferred_element_type=F32)


def _dot_nt(a, b):
    return lax.dot_general(a, b, (((1,), (1,)), ((), ())), preferred_element_type=F32)


def _cast_kernel(w_ref, o_ref):
    o_ref[...] = w_ref[...].astype(o_ref.dtype)


def _cast_cols(w, col_off, n_cols):
    depth, k, _ = w.shape
    tk, tn = 2048, 1024
    return pl.pallas_call(
        _cast_kernel,
        grid=(depth, k // tk, n_cols // tn),
        in_specs=[pl.BlockSpec((None, tk, tn), lambda l, r, j: (l, r, col_off // tn + j))],
        out_specs=pl.BlockSpec((None, tk, tn), lambda l, r, j: (l, r, j)),
        out_shape=jax.ShapeDtypeStruct((depth, k, n_cols), BF16),
        compiler_params=_cparams(3),
        name="cast_cols",
    )(w)


def _mod_kernel(c_ref, w_ref, b_ref, o_ref):
    s = _silu(c_ref[...]).astype(BF16)
    o_ref[...] = _dot(s, w_ref[...].astype(BF16)) + b_ref[...]


def _modulation(cc, w_mod, b_mod):
    tn = 512
    return pl.pallas_call(
        _mod_kernel,
        grid=(DEPTH, 3 * D_MODEL // tn),
        in_specs=[pl.BlockSpec((8, D_MODEL), lambda l, j: (0, 0)),
                  pl.BlockSpec((None, D_MODEL, tn), lambda l, j: (l, 0, j)),
                  pl.BlockSpec((None, 1, tn), lambda l, j: (l, 0, j))],
        out_specs=pl.BlockSpec((None, 8, tn), lambda l, j: (l, 0, j)),
        out_shape=jax.ShapeDtypeStruct((DEPTH, 8, 3 * D_MODEL), F32),
        compiler_params=_cparams(2),
        name="modulation",
    )(cc, w_mod, b_mod.reshape(DEPTH, 1, 3 * D_MODEL))


def _adaln_kernel(x_ref, g_ref, sh_ref, sc_ref, o_ref):
    x = x_ref[...]
    ms = jnp.mean(x * x, axis=-1, keepdims=True)
    y = x * lax.rsqrt(ms + EPS) * g_ref[...]
    o_ref[...] = (y * (1.0 + sc_ref[...]) + sh_ref[...]).astype(o_ref.dtype)


def _adaln(x2d, norm_g, mods, layer, row_of_tile, tm):
    m = x2d.shape[0]
    return pl.pallas_call(
        _adaln_kernel,
        grid=(m // tm,),
        in_specs=[pl.BlockSpec((tm, D_MODEL), lambda i: (i, 0)),
                  pl.BlockSpec((None, 1, D_MODEL), lambda i: (layer, 0, 0)),
                  pl.BlockSpec((None, None, 1, D_MODEL), lambda i: (layer, row_of_tile(i), 0, 0)),
                  pl.BlockSpec((None, None, 1, D_MODEL), lambda i: (layer, row_of_tile(i), 0, 1))],
        out_specs=pl.BlockSpec((tm, D_MODEL), lambda i: (i, 0)),
        out_shape=jax.ShapeDtypeStruct((m, D_MODEL), BF16),
        compiler_params=_cparams(1),
        name="adaln",
    )(x2d, norm_g.reshape(DEPTH, 1, D_MODEL), mods, mods)


def _proj_kernel(a_ref, w_ref, o_ref, *, act):
    o_ref[...] = _act(_dot(a_ref[...], w_ref[...]), act).astype(o_ref.dtype)


def _proj(h, w_in, layer, col_off, n_cols, act, tm):
    m, k = h.shape
    tn = 1024
    return pl.pallas_call(
        functools.partial(_proj_kernel, act=act),
        grid=(n_cols // tn, m // tm),
        in_specs=[pl.BlockSpec((tm, k), lambda j, i: (i, 0)),
                  pl.BlockSpec((None, k, tn), lambda j, i: (layer, 0, col_off // tn + j))],
        out_specs=pl.BlockSpec((tm, tn), lambda j, i: (i, j)),
        out_shape=jax.ShapeDtypeStruct((m, n_cols), BF16),
        compiler_params=_cparams(2),
        name="in_proj",
    )(h, w_in)


def _rms_head(x, g):
    ms = jnp.mean(x * x, axis=-1, keepdims=True)
    return x * lax.rsqrt(ms + EPS) * g


def _rope(y, cos, sin_signed):
    lane = lax.broadcasted_iota(jnp.int32, y.shape, 1)
    partner = jnp.where((lane & 32) == 0, pltpu.roll(y, 96, 1), pltpu.roll(y, 32, 1))
    return y * cos + partner * sin_signed


def _rope_tables():
    rows = SEQ // GRID_W
    row = np.repeat(np.arange(rows, dtype=np.float64), GRID_W)
    col = np.tile(np.arange(GRID_W, dtype=np.float64), rows)
    axis_dim = HEAD_DIM // 2
    inv_freq = ROPE_BASE ** (-np.arange(0, axis_dim, 2, dtype=np.float64) / axis_dim)
    ang_r = row[:, None] * inv_freq[None, :]
    ang_c = col[:, None] * inv_freq[None, :]
    ang = np.concatenate([ang_r, ang_r, ang_c, ang_c], axis=-1)
    sign = np.where((np.arange(HEAD_DIM) & 32) == 0, -1.0, 1.0)
    return jnp.asarray(np.concatenate([np.cos(ang), np.sin(ang) * sign[None, :]], axis=1), F32)


ATTN_SCALE = HEAD_DIM ** -0.5
GROUP_W = GROUP * HEAD_DIM
N_BLOCKS = SEQ // BLOCK
MG_TM = 1024
MG_TN = 768
MG_PARTS = 4
UNITS_PER_STEP = 2
assert (MG_TM // BLOCK) * N_KV_HEADS == UNITS_PER_STEP * (N_BRANCH * D_MODEL // MG_TN)


def _attn_scores(q_ref, kp, kc, kn, kx, tp, tc, tn, qg, kg, first, last):
    def kprep(k, t):
        return _rope(_rms_head(k.astype(F32), kg), t[:, :HEAD_DIM], t[:, HEAD_DIM:]).astype(BF16)

    kp, kc, kn = kprep(kp, tp), kprep(kc, tc), kprep(kn, tn)
    kx = _rms_head(kx.astype(F32), kg).astype(BF16)
    q = jnp.concatenate(
        [(_rope(_rms_head(q_ref[:, j * HEAD_DIM:(j + 1) * HEAD_DIM].astype(F32), qg), tc[:, :HEAD_DIM], tc[:, HEAD_DIM:])
          * ATTN_SCALE).astype(BF16) for j in range(GROUP)], axis=0)
    rows = GROUP * BLOCK
    qi = lax.broadcasted_iota(jnp.int32, (rows, BLOCK), 0) & (BLOCK - 1)
    col = lax.broadcasted_iota(jnp.int32, (rows, BLOCK), 1)
    s_p = jnp.where(col >= qi + jnp.where(first, BLOCK, 0), _dot_nt(q, kp), NEG_INF)
    s_c = _dot_nt(q, kc)
    s_n = jnp.where(col <= qi - jnp.where(last, BLOCK, 0), _dot_nt(q, kn), NEG_INF)
    s_x = _dot_nt(q, kx)
    return s_p, s_c, s_n, s_x


def _attn_finish(scores, vp, vc, vn, vx, ag_ref, sinks, o_ref):
    s_p, s_c, s_n, s_x = scores
    rows = GROUP * BLOCK
    rgrp = lax.broadcasted_iota(jnp.int32, (rows, 1), 0) >> 7
    sink = jnp.full((rows, 1), sinks[0], F32)
    for j in range(1, GROUP):
        sink = jnp.where(rgrp == j, sinks[j], sink)
    mx = jnp.maximum(jnp.maximum(s_p, s_c), s_n)
    for c in range(CTX_LEN // BLOCK):
        mx = jnp.maximum(mx, s_x[:, c * BLOCK:(c + 1) * BLOCK])
    m = jnp.maximum(jnp.max(mx, axis=-1, keepdims=True), sink)

    def pv(s, v):
        p = jnp.exp((s - m).astype(BF16))
        return _dot(p, jnp.concatenate([v, jnp.ones_like(v)], axis=1))

    acc = pv(s_p, vp) + pv(s_c, vc) + pv(s_n, vn) + pv(s_x, vx)
    den = acc[:, HEAD_DIM:HEAD_DIM + 1] + jnp.exp(sink - m)
    o = acc[:, :HEAD_DIM] * (1.0 / den)
    for j in range(GROUP):
        cs = slice(j * HEAD_DIM, (j + 1) * HEAD_DIM)
        o_ref[:, cs] = (o[j * BLOCK:(j + 1) * BLOCK] * ag_ref[:, cs].astype(F32)).astype(o_ref.dtype)


def _gate_attn_kernel(sink_ref, h_ref, w_ref, qg_ref, kg_ref, q_ref, ag_ref, kp_ref, kc_ref, kn_ref,
                      vp_ref, vc_ref, vn_ref, kx_ref, vx_ref, tp_ref, tc_ref, tn_ref, mg_ref, o_ref):
    i = pl.program_id(0)
    j = pl.program_id(1)
    steps_per_block = N_KV_HEADS // UNITS_PER_STEP
    blk = lax.rem(i, SEQ // MG_TM) * (MG_TM // BLOCK) + j // steps_per_block
    first = blk == 0
    last = blk == N_BLOCKS - 1
    qg = qg_ref[...]
    kg = kg_ref[...]
    tp, tc, tn = tp_ref[...], tc_ref[...], tn_ref[...]

    def gate_rows(part):
        rs = slice(part * MG_TM // MG_PARTS, (part + 1) * MG_TM // MG_PARTS)
        mg_ref[rs, :] = jax.nn.sigmoid(_dot(h_ref[rs, :], w_ref[...])).astype(mg_ref.dtype)

    gate_rows(0)
    scores = []
    for u in range(UNITS_PER_STEP):
        hs = slice(u * HEAD_DIM, (u + 1) * HEAD_DIM)
        scores.append(_attn_scores(q_ref.at[:, u * GROUP_W:(u + 1) * GROUP_W], kp_ref[:, hs], kc_ref[:, hs],
                                   kn_ref[:, hs], kx_ref[:, hs], tp, tc, tn, qg, kg, first, last))
    gate_rows(1)
    for u in range(UNITS_PER_STEP):
        grp = lax.rem(j, steps_per_block) * UNITS_PER_STEP + u
        hs = slice(u * HEAD_DIM, (u + 1) * HEAD_DIM)
        sinks = [sink_ref[grp * GROUP + g] for g in range(GROUP)]
        _attn_finish(scores[u], vp_ref[:, hs], vc_ref[:, hs], vn_ref[:, hs], vx_ref[:, hs],
                     ag_ref.at[:, u * GROUP_W:(u + 1) * GROUP_W], sinks, o_ref.at[u])
    for part in range(2, MG_PARTS):
        gate_rows(part)


def _gates_and_attention(h, w_mg, layer, q, kv, ag, kv_ctx, sink, q_g, k_g, rope_tab):
    m = h.shape[0]
    upb = UNITS_PER_STEP
    pairs = N_KV_HEADS // upb
    blocks_per_tile = MG_TM // BLOCK
    tiles_per_seq = SEQ // MG_TM

    def blk(i, j):
        return lax.rem(i, tiles_per_seq) * blocks_per_tile + j // pairs

    def seq0(i):
        return (i // tiles_per_seq) * N_BLOCKS

    def cur(i, j):
        return seq0(i) + blk(i, j)

    def prev(i, j):
        return seq0(i) + jnp.maximum(blk(i, j) - 1, 0)

    def nxt(i, j):
        return seq0(i) + jnp.minimum(blk(i, j) + 1, N_BLOCKS - 1)

    def pair(j):
        return lax.rem(j, pairs)

    kw = upb * HEAD_DIM
    k_spec = lambda rowf: pl.BlockSpec((BLOCK, kw), lambda i, j: (rowf(i, j), pair(j)))
    v_spec = lambda rowf: pl.BlockSpec((BLOCK, kw), lambda i, j: (rowf(i, j), pairs + pair(j)))
    t_spec = lambda rowf: pl.BlockSpec((BLOCK, 2 * HEAD_DIM), lambda i, j: (rowf(i, j) - seq0(i), 0))
    in_specs = [
        pl.BlockSpec(memory_space=pltpu.SMEM),
        pl.BlockSpec((MG_TM, D_MODEL), lambda i, j: (i, 0)),
        pl.BlockSpec((None, D_MODEL, MG_TN), lambda i, j: (layer, 0, j)),
        pl.BlockSpec((1, HEAD_DIM), lambda i, j: (0, 0)),
        pl.BlockSpec((1, HEAD_DIM), lambda i, j: (0, 0)),
        pl.BlockSpec((BLOCK, upb * GROUP_W), lambda i, j: (cur(i, j), pair(j))),
        pl.BlockSpec((BLOCK, upb * GROUP_W), lambda i, j: (cur(i, j), pair(j))),
        k_spec(prev), k_spec(cur), k_spec(nxt),
        v_spec(prev), v_spec(cur), v_spec(nxt),
        pl.BlockSpec((CTX_LEN, kw), lambda i, j: (i // tiles_per_seq, pair(j))),
        pl.BlockSpec((CTX_LEN, kw), lambda i, j: (i // tiles_per_seq, pairs + pair(j))),
        t_spec(prev), t_spec(cur), t_spec(nxt),
    ]
    out_specs = [pl.BlockSpec((MG_TM, MG_TN), lambda i, j: (i, j)),
                 pl.BlockSpec((upb, BLOCK, GROUP_W), lambda i, j: (pair(j), cur(i, j), 0))]
    return pl.pallas_call(
        _gate_attn_kernel,
        grid=(m // MG_TM, N_BRANCH * D_MODEL // MG_TN),
        in_specs=in_specs,
        out_specs=out_specs,
        out_shape=[jax.ShapeDtypeStruct((m, N_BRANCH * D_MODEL), BF16),
                   jax.ShapeDtypeStruct((N_KV_HEADS, m, GROUP_W), BF16)],
        compiler_params=_cparams(2),
        name="gates_and_attention",
    )(sink, h, w_mg, q_g, k_g, q, ag, kv, kv, kv, kv, kv, kv, kv_ctx, kv_ctx, rope_tab, rope_tab, rope_tab)


def _ctx_attn_kernel(sink_ref, q_ref, kv_ref, ag_ref, qg_ref, kg_ref, o_ref):
    qg = qg_ref[...]
    kg = kg_ref[...]
    for h in range(N_KV_HEADS):
        kx = _rms_head(kv_ref[:, h * HEAD_DIM:(h + 1) * HEAD_DIM].astype(F32), kg).astype(BF16)
        v = kv_ref[:, KV_W + h * HEAD_DIM:KV_W + (h + 1) * HEAD_DIM]
        for j in range(GROUP):
            hh = h * GROUP + j
            cs = slice(hh * HEAD_DIM, (hh + 1) * HEAD_DIM)
            qn = (_rms_head(q_ref[:, cs].astype(F32), qg) * ATTN_SCALE).astype(BF16)
            s = _dot_nt(qn, kx)
            sink = sink_ref[hh]
            m = jnp.maximum(jnp.max(s, axis=-1, keepdims=True), sink)
            p = jnp.exp(s - m)
            den = jnp.sum(p, axis=-1, keepdims=True) + jnp.exp(sink - m)
            o = _dot(p.astype(BF16), v) * (1.0 / den)
            o_ref[h, :, j * HEAD_DIM:(j + 1) * HEAD_DIM] = (o * ag_ref[:, cs].astype(F32)).astype(o_ref.dtype)


def _context_attention(q, kv, ag, sink, q_g, k_g):
    return pl.pallas_call(
        _ctx_attn_kernel,
        grid=(BATCH,),
        in_specs=[pl.BlockSpec(memory_space=pltpu.SMEM),
                  pl.BlockSpec((CTX_LEN, ATTN_W), lambda b: (b, 0)),
                  pl.BlockSpec((CTX_LEN, 2 * KV_W), lambda b: (b, 0)),
                  pl.BlockSpec((CTX_LEN, ATTN_W), lambda b: (b, 0)),
                  pl.BlockSpec((1, HEAD_DIM), lambda b: (0, 0)),
                  pl.BlockSpec((1, HEAD_DIM), lambda b: (0, 0))],
        out_specs=pl.BlockSpec((N_KV_HEADS, CTX_LEN, GROUP_W), lambda b: (0, b, 0)),
        out_shape=jax.ShapeDtypeStruct((N_KV_HEADS, BATCH * CTX_LEN, GROUP_W), BF16),
        compiler_params=_cparams(1),
        name="context_attention",
    )(sink, q, kv, ag, q_g, k_g)


def _table(values):
    return jnp.asarray(values, F32).astype(BF16)


def _channel_dft_tables():
    n = FOURIER_GROUP_W
    ang = 2.0 * np.pi * np.outer(np.arange(n), np.arange(n)) / n
    s = n ** -0.5
    return _table(np.cos(ang) * s), _table(np.sin(ang) * s)


def _stage1_table():
    ang = 2.0 * np.pi * np.outer(np.arange(FFT_N2), np.arange(FFT_N2)) / FFT_N2
    s = FFT_N2 ** -0.5
    return _table(np.concatenate([np.cos(ang), -np.sin(ang)], axis=0) * s)


def _stage2_table():
    k2 = np.arange(FFT_N2)[:, None, None]
    k1 = np.arange(FFT_N1)[None, :, None]
    l1 = np.arange(FFT_N1)[None, None, :]
    ang = 2.0 * np.pi * ((l1 * (k2 + FFT_N2 * k1)) % SEQ) / SEQ
    tr = np.cos(ang) * FFT_N1 ** -0.5
    ti = -np.sin(ang) * FFT_N1 ** -0.5
    top = np.concatenate([tr, -ti], axis=2)
    bot = np.concatenate([ti, tr], axis=2)
    return _table(np.concatenate([top, bot], axis=1))


def _interleave_table():
    n = 8 * FFT_KB
    p = np.zeros((n, n), np.float32)
    for b in range(8):
        for j in range(FFT_KB):
            p[b * FFT_KB + j, j * 8 + b] = 1.0
    return _table(p)


def _fft1_kernel(d_ref, u_ref, o_ref):
    o_ref[...] = _dot(d_ref[...], u_ref[...]).astype(o_ref.dtype)


def _fft2_kernel(z_ref, t_ref, cc_ref, sc_ref, w_ref, p_ref, fg_ref, o_ref, ybuf, rbuf, ibuf):
    for j in range(FFT_KB):
        zst = jnp.concatenate([z_ref[0, j], z_ref[1, j]], axis=0)
        y = _dot(t_ref[j], zst)
        rbuf[j * FFT_N1:(j + 1) * FFT_N1, :] = y[:FFT_N1].astype(BF16)
        ibuf[j * FFT_N1:(j + 1) * FFT_N1, :] = y[FFT_N1:].astype(BF16)
    a = _dot(rbuf[...], cc_ref[...]) + _dot(ibuf[...], sc_ref[...])
    ybuf[...] = _dot(a.astype(BF16), w_ref[...])
    perm = p_ref[...]
    for a8 in range(FFT_N1 // 8):
        blk = jnp.concatenate([ybuf[j * FFT_N1 + 8 * a8:j * FFT_N1 + 8 * a8 + 8, :] for j in range(FFT_KB)], axis=0)
        nat = _dot(perm, blk.astype(BF16))
        gate = fg_ref[8 * a8:8 * a8 + 8].reshape(8 * FFT_KB, FOURIER_GROUP_W).astype(F32)
        o_ref[8 * a8:8 * a8 + 8] = (nat * gate).astype(o_ref.dtype).reshape(8, FFT_KB, FOURIER_GROUP_W)


def _latent_fourier(u, fg, w_mix, layer, tabs):
    d1, t2, cc, sc, perm = tabs
    cols = FFT_N1 * FOURIER_W
    tn = 16384
    z = pl.pallas_call(
        _fft1_kernel,
        grid=(BATCH, cols // tn),
        in_specs=[pl.BlockSpec((2 * FFT_N2, FFT_N2), lambda b, j: (0, 0)),
                  pl.BlockSpec((None, FFT_N2, tn), lambda b, j: (b, 0, j))],
        out_specs=pl.BlockSpec((None, 2 * FFT_N2, tn), lambda b, j: (b, 0, j)),
        out_shape=jax.ShapeDtypeStruct((BATCH, 2 * FFT_N2, cols), BF16),
        compiler_params=_cparams(2),
        name="fourier_stage1",
    )(d1, u.reshape(BATCH, FFT_N2, cols))
    z = z.reshape(BATCH, 2, FFT_N2, FFT_N1, FOURIER_W)
    gw = FOURIER_GROUP_W
    out = pl.pallas_call(
        _fft2_kernel,
        grid=(BATCH, FFT_N2 // FFT_KB, FOURIER_GROUPS),
        in_specs=[pl.BlockSpec((None, 2, FFT_KB, FFT_N1, gw), lambda b, t, g: (b, 0, t, 0, g)),
                  pl.BlockSpec((FFT_KB, 2 * FFT_N1, 2 * FFT_N1), lambda b, t, g: (t, 0, 0)),
                  pl.BlockSpec((gw, gw), lambda b, t, g: (0, 0)),
                  pl.BlockSpec((gw, gw), lambda b, t, g: (0, 0)),
                  pl.BlockSpec((None, None, gw, gw), lambda b, t, g: (layer, g, 0, 0)),
                  pl.BlockSpec((8 * FFT_KB, 8 * FFT_KB), lambda b, t, g: (0, 0)),
                  pl.BlockSpec((None, FFT_N1, FFT_KB, gw), lambda b, t, g: (b, 0, t, g))],
        out_specs=pl.BlockSpec((None, FFT_N1, FFT_KB, gw), lambda b, t, g: (b, 0, t, g)),
        out_shape=jax.ShapeDtypeStruct((BATCH, FFT_N1, FFT_N2, FOURIER_W), BF16),
        scratch_shapes=[pltpu.VMEM((FFT_KB * FFT_N1, gw), F32),
                        pltpu.VMEM((FFT_KB * FFT_N1, gw), BF16), pltpu.VMEM((FFT_KB * FFT_N1, gw), BF16)],
        compiler_params=_cparams(3),
        name="fourier_stage2",
    )(z, t2, cc, sc, w_mix, perm, fg.reshape(BATCH, FFT_N1, FFT_N2, FOURIER_W))
    return out.reshape(BATCH * SEQ, FOURIER_W)


def _ctx_dft_table():
    n = CTX_LEN
    ang = 2.0 * np.pi * np.outer(np.arange(n), np.arange(n)) / n
    s = n ** -0.5
    return _table(np.concatenate([np.cos(ang), -np.sin(ang)], axis=1) * s)


def _ctx_fourier_kernel(u_ref, fg_ref, cls_ref, cc_ref, sc_ref, w_ref, o_ref):
    u = u_ref[...]
    st = jnp.concatenate([_dot(u, cc_ref[...]), _dot(u, sc_ref[...])], axis=0).astype(BF16)
    f = _dot(cls_ref[...], st)
    y = _dot(f.astype(BF16), w_ref[...])
    o_ref[...] = (y * fg_ref[...].astype(F32)).astype(o_ref.dtype)


def _context_fourier(u, fg, w_mix, layer, cls, cc, sc):
    gw = FOURIER_GROUP_W
    return pl.pallas_call(
        _ctx_fourier_kernel,
        grid=(BATCH, FOURIER_GROUPS),
        in_specs=[pl.BlockSpec((CTX_LEN, gw), lambda b, g: (b, g)),
                  pl.BlockSpec((CTX_LEN, gw), lambda b, g: (b, g)),
                  pl.BlockSpec((CTX_LEN, 2 * CTX_LEN), lambda b, g: (0, 0)),
                  pl.BlockSpec((gw, gw), lambda b, g: (0, 0)),
                  pl.BlockSpec((gw, gw), lambda b, g: (0, 0)),
                  pl.BlockSpec((None, None, gw, gw), lambda b, g: (layer, g, 0, 0))],
        out_specs=pl.BlockSpec((CTX_LEN, gw), lambda b, g: (b, g)),
        out_shape=jax.ShapeDtypeStruct((BATCH * CTX_LEN, FOURIER_W), BF16),
        compiler_params=_cparams(2),
        name="context_fourier",
    )(u, fg, cls, cc, sc, w_mix)


def _conv_norm_act(u_ext, dww_ref, dwb_ref, lng_ref, lnb_ref, ubuf, tr):
    n_slab = CONV_W // LANES
    for c in range(n_slab):
        ubuf[c] = u_ext[:, c * LANES:(c + 1) * LANES]
    first = CONV_HALO - CONV_K // 2
    ys = []
    for c in range(n_slab):
        cs = slice(c * LANES, (c + 1) * LANES)
        acc = ubuf[c, first:first + tr, :] * dww_ref[0:1, cs]
        for k in range(1, CONV_K):
            acc = acc + ubuf[c, first + k:first + k + tr, :] * dww_ref[k:k + 1, cs]
        ys.append(acc)
    y = jnp.concatenate(ys, axis=1) + dwb_ref[...]
    mu = jnp.mean(y, axis=-1, keepdims=True)
    yc = y - mu
    var = jnp.mean(yc * yc, axis=-1, keepdims=True)
    return _silu(yc * lax.rsqrt(var + EPS) * lng_ref[...] + lnb_ref[...]).astype(BF16)


def _conv_kernel(a_ref, b_ref, ap_ref, bp_ref, an_ref, bn_ref, cg_ref, dww_ref, dwb_ref, lng_ref, lnb_ref,
                 wpw_ref, o_ref, ubuf, *, tiles_per_seq, tr):
    ti = lax.rem(pl.program_id(0), tiles_per_seq)
    u = a_ref[...].astype(F32) * b_ref[...].astype(F32)
    up = jnp.where(ti > 0, ap_ref[...].astype(F32) * bp_ref[...].astype(F32), 0.0)
    un = jnp.where(ti < tiles_per_seq - 1, an_ref[...].astype(F32) * bn_ref[...].astype(F32), 0.0)
    z = _conv_norm_act(jnp.concatenate([up, u, un], axis=0), dww_ref, dwb_ref, lng_ref, lnb_ref, ubuf, tr)
    o_ref[...] = (_dot(z, wpw_ref[...]) * cg_ref[...].astype(F32)).astype(o_ref.dtype)


def _conv_specs(m, tr, layer, tile_of):
    hb = tr // CONV_HALO
    n_hb = m // CONV_HALO
    tile = pl.BlockSpec((tr, CONV_W), lambda *idx: (tile_of(*idx), 0))
    halo_p = pl.BlockSpec((CONV_HALO, CONV_W), lambda *idx: (jnp.maximum(tile_of(*idx) * hb - 1, 0), 0))
    halo_n = pl.BlockSpec((CONV_HALO, CONV_W), lambda *idx: (jnp.minimum(tile_of(*idx) * hb + hb, n_hb - 1), 0))
    vec = pl.BlockSpec((None, 1, CONV_W), lambda *idx: (layer, 0, 0))
    specs = [tile, tile, halo_p, halo_p, halo_n, halo_n, tile,
             pl.BlockSpec((None, CONV_K, CONV_W), lambda *idx: (layer, 0, 0)),
             vec, vec, vec,
             pl.BlockSpec((None, CONV_W, CONV_W), lambda *idx: (layer, 0, 0))]
    return specs, tile


def _conv_args(a, b_sig, cg, dw_w, dw_b, ln_g, ln_b, w_pw):
    return (a, b_sig, a, b_sig, a, b_sig, cg, dw_w,
            dw_b.reshape(DEPTH, 1, CONV_W), ln_g.reshape(DEPTH, 1, CONV_W), ln_b.reshape(DEPTH, 1, CONV_W), w_pw)


def _conformer(a, b_sig, cg, dw_w, dw_b, ln_g, ln_b, w_pw, layer, seq_len, tr):
    m = a.shape[0]
    specs, tile = _conv_specs(m, tr, layer, lambda i: i)
    return pl.pallas_call(
        functools.partial(_conv_kernel, tiles_per_seq=seq_len // tr, tr=tr),
        grid=(m // tr,),
        in_specs=specs,
        out_specs=tile,
        out_shape=jax.ShapeDtypeStruct((m, CONV_W), BF16),
        scratch_shapes=[pltpu.VMEM((CONV_W // LANES, tr + 2 * CONV_HALO, LANES), F32)],
        compiler_params=_cparams(1),
        name="conformer_conv",
    )(*_conv_args(a, b_sig, cg, dw_w, dw_b, ln_g, ln_b, w_pw))


def _merge_kernel(a_ref, f_ref, c_ref, g0_ref, g1_ref, g2_ref, wa_ref, wf_ref, wc_ref, o_ref):
    y_attn = _dot(a_ref[0], wa_ref[0:GROUP_W, :])
    for g in range(1, N_KV_HEADS):
        y_attn = y_attn + _dot(a_ref[g], wa_ref[g * GROUP_W:(g + 1) * GROUP_W, :])
    merged = (g0_ref[...].astype(F32) * y_attn
              + g1_ref[...].astype(F32) * _dot(f_ref[...], wf_ref[...])
              + g2_ref[...].astype(F32) * _dot(c_ref[...], wc_ref[...]))
    o_ref[...] = merged.astype(o_ref.dtype)


def _merge(attn, four, conv, mg, w_attn_up, w_four_up, w_conv_up, layer, tm):
    m = four.shape[0]
    tn = 1024
    nj = D_MODEL // tn
    return pl.pallas_call(
        _merge_kernel,
        grid=(nj, m // tm),
        in_specs=[pl.BlockSpec((N_KV_HEADS, tm, GROUP_W), lambda j, i: (0, i, 0)),
                  pl.BlockSpec((tm, FOURIER_W), lambda j, i: (i, 0)),
                  pl.BlockSpec((tm, CONV_W), lambda j, i: (i, 0)),
                  pl.BlockSpec((tm, tn), lambda j, i: (i, j)),
                  pl.BlockSpec((tm, tn), lambda j, i: (i, nj + j)),
                  pl.BlockSpec((tm, tn), lambda j, i: (i, 2 * nj + j)),
                  pl.BlockSpec((None, ATTN_W, tn), lambda j, i: (layer, 0, j)),
                  pl.BlockSpec((None, FOURIER_W, tn), lambda j, i: (layer, 0, j)),
                  pl.BlockSpec((None, CONV_W, tn), lambda j, i: (layer, 0, j))],
        out_specs=pl.BlockSpec((tm, tn), lambda j, i: (i, j)),
        out_shape=jax.ShapeDtypeStruct((m, D_MODEL), BF16),
        compiler_params=_cparams(2),
        name="branch_merge",
    )(attn, four, conv, mg, mg, mg, w_attn_up, w_four_up, w_conv_up)


def _out_kernel(a_ref, w_ref, x_ref, gt_ref, o_ref):
    o_ref[...] = x_ref[...] + gt_ref[...] * _dot(a_ref[...], w_ref[...])


def _out_proj(merged, w_out, x2d, mods, layer, row_of_tile, tm):
    m = merged.shape[0]
    tn = 1024
    return pl.pallas_call(
        _out_kernel,
        grid=(D_MODEL // tn, m // tm),
        in_specs=[pl.BlockSpec((tm, D_MODEL), lambda j, i: (i, 0)),
                  pl.BlockSpec((None, D_MODEL, tn), lambda j, i: (layer, 0, j)),
                  pl.BlockSpec((tm, tn), lambda j, i: (i, j)),
                  pl.BlockSpec((None, None, 1, tn), lambda j, i: (layer, row_of_tile(i), 0, 2 * (D_MODEL // tn) + j))],
        out_specs=pl.BlockSpec((tm, tn), lambda j, i: (i, j)),
        out_shape=jax.ShapeDtypeStruct((m, D_MODEL), F32),
        compiler_params=_cparams(2),
        name="out_proj",
    )(merged, w_out, x2d, mods)


def _branches_and_merge(h, w, layer, tm, gates_and_attention, fourier, conv_seq, conv_tr):
    w_in = w["w_in"]
    conv_w = (w["conv_dw_w"], w["conv_dw_b"], w["conv_ln_g"], w["conv_ln_b"], w["w_conv_pw"])
    ca = _proj(h, w_in, layer, CA_OFF, CONV_W, "none", tm)
    cb = _proj(h, w_in, layer, CB_OFF, CONV_W, "sigmoid", tm)
    cg = _proj(h, w_in, layer, CG_OFF, CONV_W, "silu", tm)
    conv = _conformer(ca, cb, cg, *conv_w, layer, conv_seq, conv_tr)
    q = _proj(h, w_in, layer, Q_OFF, ATTN_W, "none", tm)
    kv = _proj(h, w_in, layer, K_OFF, 2 * KV_W, "none", tm)
    ag = _proj(h, w_in, layer, AG_OFF, ATTN_W, "silu", tm)
    fu = _proj(h, w_in, layer, F_OFF, FOURIER_W, "none", tm)
    fg = _proj(h, w_in, layer, FG_OFF, FOURIER_W, "silu", tm)
    mg, attn = gates_and_attention(h, q, kv, ag)
    four = fourier(fu, fg)
    merged = _merge(attn, four, conv, mg, w["w_attn_up"], w["w_fourier_up"], w["w_conv_up"], layer, min(tm, 512))
    return merged, kv


@jax.jit
def _block(x, c, ctx, c_ctx, norm_g, w_mod, b_mod, w_in, q_norm_g, k_norm_g, attn_sink, w_attn_up,
           w_fourier_mix, w_fourier_up, conv_dw_w, conv_dw_b, conv_ln_g, conv_ln_b, w_conv_pw, w_conv_up, w_out):
    w = dict(w_in=_cast_cols(w_in, 0, MG_OFF), w_mg=_cast_cols(w_in, MG_OFF, N_BRANCH * D_MODEL),
             w_attn_up=w_attn_up.astype(BF16), w_fourier_up=w_fourier_up.astype(BF16),
             w_conv_pw=w_conv_pw.astype(BF16), w_conv_up=w_conv_up.astype(BF16),
             conv_dw_w=conv_dw_w, conv_dw_b=conv_dw_b, conv_ln_g=conv_ln_g, conv_ln_b=conv_ln_b)
    w_out_b = w_out.astype(BF16)
    w_mix_b = w_fourier_mix.astype(BF16)

    rope_tab = _rope_tables()
    cc_t, sc_t = _channel_dft_tables()
    lat_tabs = (_stage1_table(), _stage2_table(), cc_t, sc_t, _interleave_table())
    cls_t = _ctx_dft_table()

    cond = jnp.concatenate([c, c_ctx[None, :], jnp.zeros((8 - BATCH - 1, D_MODEL), F32)], axis=0)
    mods = _modulation(cond, w_mod, b_mod).reshape(DEPTH, 8, 1, 3 * D_MODEL)
    ctx_row = BATCH

    x2d = x.reshape(BATCH * SEQ, D_MODEL)
    ctx2d = ctx.reshape(BATCH * CTX_LEN, D_MODEL)
    lat_tm = 1024
    ctx_tm = BATCH * CTX_LEN
    norm_tm = 512
    lat_tiles_per_batch = SEQ // norm_tm

    for l in range(DEPTH):
        sink = attn_sink[l]
        qg = q_norm_g[l].reshape(1, HEAD_DIM)
        kg = k_norm_g[l].reshape(1, HEAD_DIM)

        h_ctx = _adaln(ctx2d, norm_g, mods, l, lambda i: ctx_row, norm_tm)
        if l < DEPTH - 1:
            merged_c, kv_c = _branches_and_merge(
                h_ctx, w, l, ctx_tm,
                gates_and_attention=lambda h_, q, kv, ag: (
                    _proj(h_, w["w_mg"], l, 0, N_BRANCH * D_MODEL, "sigmoid", ctx_tm),
                    _context_attention(q, kv, ag, sink, qg, kg)),
                fourier=lambda fu, fg: _context_fourier(fu, fg, w_mix_b, l, cls_t, cc_t, sc_t),
                conv_seq=CTX_LEN, conv_tr=CTX_LEN)
            ctx_next = _out_proj(merged_c, w_out_b, ctx2d, mods, l, lambda i: ctx_row, ctx_tm)
        else:
            kv_c = _proj(h_ctx, w["w_in"], l, K_OFF, 2 * KV_W, "none", ctx_tm)
            ctx_next = ctx2d

        h = _adaln(x2d, norm_g, mods, l, lambda i: i // lat_tiles_per_batch, norm_tm)
        merged, _ = _branches_and_merge(
            h, w, l, lat_tm,
            gates_and_attention=lambda h_, q, kv, ag: _gates_and_attention(
                h_, w["w_mg"], l, q, kv, ag, kv_c, sink, qg, kg, rope_tab),
            fourier=lambda fu, fg: _latent_fourier(fu, fg, w_mix_b, l, lat_tabs),
            conv_seq=SEQ, conv_tr=256)
        out_tm = 1024
        x2d = _out_proj(merged, w_out_b, x2d, mods, l, lambda i: i // (SEQ // out_tm), out_tm)
        ctx2d = ctx_next
    return x2d.reshape(BATCH, SEQ, D_MODEL)


def kernel(x, c, ctx, c_ctx, norm_g, w_mod, b_mod, w_in, q_norm_g, k_norm_g, attn_sink, w_attn_up, w_fourier_mix,
           w_fourier_up, conv_dw_w, conv_dw_b, conv_ln_g, conv_ln_b, w_conv_pw, w_conv_up, w_out):
    return _block(x, c, ctx, c_ctx, norm_g, w_mod, b_mod, w_in, q_norm_g, k_norm_g, attn_sink, w_attn_up,
                  w_fourier_mix, w_fourier_up, conv_dw_w, conv_dw_b, conv_ln_g, conv_ln_b, w_conv_pw, w_conv_up,
                  w_out)
```

```python
import functools

import numpy as np
import jax
import jax.numpy as jnp
from jax import lax
from jax.experimental import pallas as pl
from jax.experimental.pallas import tpu as pltpu

D_MODEL = 4096
BATCH = 2
SEQ = 8192
DEPTH = 2
CTX_LEN = 256
GRID_W = 64

N_HEADS = 16
N_KV_HEADS = 4
HEAD_DIM = 128
GROUP = N_HEADS // N_KV_HEADS
ATTN_W = N_HEADS * HEAD_DIM
KV_W = N_KV_HEADS * HEAD_DIM
WINDOW = 128
BLOCK = 128
ROPE_BASE = 10000.0

FOURIER_GROUPS = 4
FOURIER_GROUP_W = 256
FOURIER_W = FOURIER_GROUPS * FOURIER_GROUP_W

CONV_W = 1024
CONV_K = 31
CONV_HALO = 16

N_BRANCH = 3
EPS = 1e-6
NEG_INF = -1e30

Q_OFF = 0
K_OFF = Q_OFF + ATTN_W
AG_OFF = K_OFF + 2 * KV_W
F_OFF = AG_OFF + ATTN_W
FG_OFF = F_OFF + FOURIER_W
CA_OFF = FG_OFF + FOURIER_W
CB_OFF = CA_OFF + CONV_W
CG_OFF = CB_OFF + CONV_W
MG_OFF = CG_OFF + CONV_W
IN_W = MG_OFF + N_BRANCH * D_MODEL

FFT_N1 = 128
FFT_N2 = 64
FFT_KB = 16

V7X_VMEM_LIMIT = 56 * 1024 * 1024
LANES = 128

BF16 = jnp.bfloat16
F32 = jnp.float32


def _cparams(n_axes, vmem_limit_bytes=V7X_VMEM_LIMIT):
    return pltpu.CompilerParams(dimension_semantics=("arbitrary",) * n_axes,
                                vmem_limit_bytes=vmem_limit_bytes)


def _silu(v):
    return v * jax.nn.sigmoid(v)


def _act(v, act):
    if act == "silu":
        return _silu(v)
    if act == "sigmoid":
        return jax.nn.sigmoid(v)
    return v


def _dot(a, b):
    return jnp.dot(a, b, preferred_element_type=F32)


def _dot_nt(a, b):
    return lax.dot_general(a, b, (((1,), (1,)), ((), ())), preferred_element_type=F32)


def _cast_kernel(w_ref, o_ref):
    o_ref[...] = w_ref[...].astype(o_ref.dtype)


def _cast_cols(w, col_off, n_cols):
    depth, k, _ = w.shape
    tk, tn = 2048, 1024
    return pl.pallas_call(
        _cast_kernel,
        grid=(depth, k // tk, n_cols // tn),
        in_specs=[pl.BlockSpec((None, tk, tn), lambda l, r, j: (l, r, col_off // tn + j))],
        out_specs=pl.BlockSpec((None, tk, tn), lambda l, r, j: (l, r, j)),
        out_shape=jax.ShapeDtypeStruct((depth, k, n_cols), BF16),
        compiler_params=_cparams(3),
        name="cast_cols",
    )(w)


def _mod_kernel(c_ref, w_ref, b_ref, o_ref):
    s = _silu(c_ref[...]).astype(BF16)
    o_ref[...] = _dot(s, w_ref[...].astype(BF16)) + b_ref[...]


def _modulation(cc, w_mod, b_mod):
    tn = 512
    return pl.pallas_call(
        _mod_kernel,
        grid=(DEPTH, 3 * D_MODEL // tn),
        in_specs=[pl.BlockSpec((8, D_MODEL), lambda l, j: (0, 0)),
                  pl.BlockSpec((None, D_MODEL, tn), lambda l, j: (l, 0, j)),
                  pl.BlockSpec((None, 1, tn), lambda l, j: (l, 0, j))],
        out_specs=pl.BlockSpec((None, 8, tn), lambda l, j: (l, 0, j)),
        out_shape=jax.ShapeDtypeStruct((DEPTH, 8, 3 * D_MODEL), F32),
        compiler_params=_cparams(2),
        name="modulation",
    )(cc, w_mod, b_mod.reshape(DEPTH, 1, 3 * D_MODEL))


def _adaln_kernel(x_ref, g_ref, sh_ref, sc_ref, o_ref):
    x = x_ref[...]
    ms = jnp.mean(x * x, axis=-1, keepdims=True)
    y = x * lax.rsqrt(ms + EPS) * g_ref[...]
    o_ref[...] = (y * (1.0 + sc_ref[...]) + sh_ref[...]).astype(o_ref.dtype)


def _adaln(x2d, norm_g, mods, layer, row_of_tile, tm):
    m = x2d.shape[0]
    return pl.pallas_call(
        _adaln_kernel,
        grid=(m // tm,),
        in_specs=[pl.BlockSpec((tm, D_MODEL), lambda i: (i, 0)),
                  pl.BlockSpec((None, 1, D_MODEL), lambda i: (layer, 0, 0)),
                  pl.BlockSpec((None, None, 1, D_MODEL), lambda i: (layer, row_of_tile(i), 0, 0)),
                  pl.BlockSpec((None, None, 1, D_MODEL), lambda i: (layer, row_of_tile(i), 0, 1))],
        out_specs=pl.BlockSpec((tm, D_MODEL), lambda i: (i, 0)),
        out_shape=jax.ShapeDtypeStruct((m, D_MODEL), BF16),
        compiler_params=_cparams(1),
        name="adaln",
    )(x2d, norm_g.reshape(DEPTH, 1, D_MODEL), mods, mods)


ROW_PARTS = 4


def _row_parts(rows):
    step = rows // ROW_PARTS
    return [slice(p * step, (p + 1) * step) for p in range(ROW_PARTS)]


def _proj_kernel(a_ref, w_ref, o_ref, *, act):
    for rs in _row_parts(a_ref.shape[0]):
        o_ref[rs, :] = _act(_dot(a_ref[rs, :], w_ref[...]), act).astype(o_ref.dtype)


def _proj(h, w_in, layer, col_off, n_cols, act, tm):
    m, k = h.shape
    tn = 1024
    return pl.pallas_call(
        functools.partial(_proj_kernel, act=act),
        grid=(n_cols // tn, m // tm),
        in_specs=[pl.BlockSpec((tm, k), lambda j, i: (i, 0)),
                  pl.BlockSpec((None, k, tn), lambda j, i: (layer, 0, col_off // tn + j))],
        out_specs=pl.BlockSpec((tm, tn), lambda j, i: (i, j)),
        out_shape=jax.ShapeDtypeStruct((m, n_cols), BF16),
        compiler_params=_cparams(2),
        name="in_proj",
    )(h, w_in)


def _rms_head(x, g):
    ms = jnp.mean(x * x, axis=-1, keepdims=True)
    return x * lax.rsqrt(ms + EPS) * g


def _rope(y, cos, sin_signed):
    lane = lax.broadcasted_iota(jnp.int32, y.shape, 1)
    partner = jnp.where((lane & 32) == 0, pltpu.roll(y, 96, 1), pltpu.roll(y, 32, 1))
    return y * cos + partner * sin_signed


def _rope_tables():
    rows = SEQ // GRID_W
    row = np.repeat(np.arange(rows, dtype=np.float64), GRID_W)
    col = np.tile(np.arange(GRID_W, dtype=np.float64), rows)
    axis_dim = HEAD_DIM // 2
    inv_freq = ROPE_BASE ** (-np.arange(0, axis_dim, 2, dtype=np.float64) / axis_dim)
    ang_r = row[:, None] * inv_freq[None, :]
    ang_c = col[:, None] * inv_freq[None, :]
    ang = np.concatenate([ang_r, ang_r, ang_c, ang_c], axis=-1)
    sign = np.where((np.arange(HEAD_DIM) & 32) == 0, -1.0, 1.0)
    return jnp.asarray(np.concatenate([np.cos(ang), np.sin(ang) * sign[None, :]], axis=1), F32)


ATTN_SCALE = HEAD_DIM ** -0.5
GROUP_W = GROUP * HEAD_DIM
N_BLOCKS = SEQ // BLOCK
MG_TM = 1024
MG_TN = 768
MG_PARTS = 4
UNITS_PER_STEP = 2
assert (MG_TM // BLOCK) * N_KV_HEADS == UNITS_PER_STEP * (N_BRANCH * D_MODEL // MG_TN)


def _attn_scores(q_ref, kp, kc, kn, kx, tp, tc, tn, qg, kg, first, last):
    def kprep(k, t):
        return _rope(_rms_head(k.astype(F32), kg), t[:, :HEAD_DIM], t[:, HEAD_DIM:]).astype(BF16)

    kp, kc, kn = kprep(kp, tp), kprep(kc, tc), kprep(kn, tn)
    kx = _rms_head(kx.astype(F32), kg).astype(BF16)
    q = jnp.concatenate(
        [(_rope(_rms_head(q_ref[:, j * HEAD_DIM:(j + 1) * HEAD_DIM].astype(F32), qg), tc[:, :HEAD_DIM], tc[:, HEAD_DIM:])
          * ATTN_SCALE).astype(BF16) for j in range(GROUP)], axis=0)
    rows = GROUP * BLOCK
    qi = lax.broadcasted_iota(jnp.int32, (rows, BLOCK), 0) & (BLOCK - 1)
    col = lax.broadcasted_iota(jnp.int32, (rows, BLOCK), 1)
    s_p = jnp.where(col >= qi + jnp.where(first, BLOCK, 0), _dot_nt(q, kp), NEG_INF)
    s_c = _dot_nt(q, kc)
    s_n = jnp.where(col <= qi - jnp.where(last, BLOCK, 0), _dot_nt(q, kn), NEG_INF)
    s_x = _dot_nt(q, kx)
    return s_p, s_c, s_n, s_x


def _attn_finish(scores, vp, vc, vn, vx, ag_ref, sinks, o_ref):
    s_p, s_c, s_n, s_x = scores
    rows = GROUP * BLOCK
    rgrp = lax.broadcasted_iota(jnp.int32, (rows, 1), 0) >> 7
    sink = jnp.full((rows, 1), sinks[0], F32)
    for j in range(1, GROUP):
        sink = jnp.where(rgrp == j, sinks[j], sink)
    mx = jnp.maximum(jnp.maximum(s_p, s_c), s_n)
    for c in range(CTX_LEN // BLOCK):
        mx = jnp.maximum(mx, s_x[:, c * BLOCK:(c + 1) * BLOCK])
    m = jnp.maximum(jnp.max(mx, axis=-1, keepdims=True), sink)

    def pv(s, v):
        p = jnp.exp((s - m).astype(BF16))
        return _dot(p, jnp.concatenate([v, jnp.ones_like(v)], axis=1))

    acc = pv(s_p, vp) + pv(s_c, vc) + pv(s_n, vn) + pv(s_x, vx)
    den = acc[:, HEAD_DIM:HEAD_DIM + 1] + jnp.exp(sink - m)
    o = acc[:, :HEAD_DIM] * (1.0 / den)
    for j in range(GROUP):
        cs = slice(j * HEAD_DIM, (j + 1) * HEAD_DIM)
        o_ref[:, cs] = (o[j * BLOCK:(j + 1) * BLOCK] * ag_ref[:, cs].astype(F32)).astype(o_ref.dtype)


def _gate_attn_kernel(sink_ref, h_ref, w_ref, qg_ref, kg_ref, q_ref, ag_ref, kp_ref, kc_ref, kn_ref,
                      vp_ref, vc_ref, vn_ref, kx_ref, vx_ref, tp_ref, tc_ref, tn_ref, mg_ref, o_ref):
    i = pl.program_id(0)
    j = pl.program_id(1)
    steps_per_block = N_KV_HEADS // UNITS_PER_STEP
    blk = lax.rem(i, SEQ // MG_TM) * (MG_TM // BLOCK) + j // steps_per_block
    first = blk == 0
    last = blk == N_BLOCKS - 1
    qg = qg_ref[...]
    kg = kg_ref[...]
    tp, tc, tn = tp_ref[...], tc_ref[...], tn_ref[...]

    def gate_rows(part):
        rs = slice(part * MG_TM // MG_PARTS, (part + 1) * MG_TM // MG_PARTS)
        mg_ref[rs, :] = jax.nn.sigmoid(_dot(h_ref[rs, :], w_ref[...])).astype(mg_ref.dtype)

    gate_rows(0)
    scores = []
    for u in range(UNITS_PER_STEP):
        hs = slice(u * HEAD_DIM, (u + 1) * HEAD_DIM)
        scores.append(_attn_scores(q_ref.at[:, u * GROUP_W:(u + 1) * GROUP_W], kp_ref[:, hs], kc_ref[:, hs],
                                   kn_ref[:, hs], kx_ref[:, hs], tp, tc, tn, qg, kg, first, last))
    gate_rows(1)
    for u in range(UNITS_PER_STEP):
        grp = lax.rem(j, steps_per_block) * UNITS_PER_STEP + u
        hs = slice(u * HEAD_DIM, (u + 1) * HEAD_DIM)
        sinks = [sink_ref[grp * GROUP + g] for g in range(GROUP)]
        _attn_finish(scores[u], vp_ref[:, hs], vc_ref[:, hs], vn_ref[:, hs], vx_ref[:, hs],
                     ag_ref.at[:, u * GROUP_W:(u + 1) * GROUP_W], sinks, o_ref.at[u])
    for part in range(2, MG_PARTS):
        gate_rows(part)


def _gates_and_attention(h, w_mg, layer, q, kv, ag, kv_ctx, sink, q_g, k_g, rope_tab):
    m = h.shape[0]
    upb = UNITS_PER_STEP
    pairs = N_KV_HEADS // upb
    blocks_per_tile = MG_TM // BLOCK
    tiles_per_seq = SEQ // MG_TM

    def blk(i, j):
        return lax.rem(i, tiles_per_seq) * blocks_per_tile + j // pairs

    def seq0(i):
        return (i // tiles_per_seq) * N_BLOCKS

    def cur(i, j):
        return seq0(i) + blk(i, j)

    def prev(i, j):
        return seq0(i) + jnp.maximum(blk(i, j) - 1, 0)

    def nxt(i, j):
        return seq0(i) + jnp.minimum(blk(i, j) + 1, N_BLOCKS - 1)

    def pair(j):
        return lax.rem(j, pairs)

    kw = upb * HEAD_DIM
    k_spec = lambda rowf: pl.BlockSpec((BLOCK, kw), lambda i, j: (rowf(i, j), pair(j)))
    v_spec = lambda rowf: pl.BlockSpec((BLOCK, kw), lambda i, j: (rowf(i, j), pairs + pair(j)))
    t_spec = lambda rowf: pl.BlockSpec((BLOCK, 2 * HEAD_DIM), lambda i, j: (rowf(i, j) - seq0(i), 0))
    in_specs = [
        pl.BlockSpec(memory_space=pltpu.SMEM),
        pl.BlockSpec((MG_TM, D_MODEL), lambda i, j: (i, 0)),
        pl.BlockSpec((None, D_MODEL, MG_TN), lambda i, j: (layer, 0, j)),
        pl.BlockSpec((1, HEAD_DIM), lambda i, j: (0, 0)),
        pl.BlockSpec((1, HEAD_DIM), lambda i, j: (0, 0)),
        pl.BlockSpec((BLOCK, upb * GROUP_W), lambda i, j: (cur(i, j), pair(j))),
        pl.BlockSpec((BLOCK, upb * GROUP_W), lambda i, j: (cur(i, j), pair(j))),
        k_spec(prev), k_spec(cur), k_spec(nxt),
        v_spec(prev), v_spec(cur), v_spec(nxt),
        pl.BlockSpec((CTX_LEN, kw), lambda i, j: (i // tiles_per_seq, pair(j))),
        pl.BlockSpec((CTX_LEN, kw), lambda i, j: (i // tiles_per_seq, pairs + pair(j))),
        t_spec(prev), t_spec(cur), t_spec(nxt),
    ]
    out_specs = [pl.BlockSpec((MG_TM, MG_TN), lambda i, j: (i, j)),
                 pl.BlockSpec((upb, BLOCK, GROUP_W), lambda i, j: (pair(j), cur(i, j), 0))]
    return pl.pallas_call(
        _gate_attn_kernel,
        grid=(m // MG_TM, N_BRANCH * D_MODEL // MG_TN),
        in_specs=in_specs,
        out_specs=out_specs,
        out_shape=[jax.ShapeDtypeStruct((m, N_BRANCH * D_MODEL), BF16),
                   jax.ShapeDtypeStruct((N_KV_HEADS, m, GROUP_W), BF16)],
        compiler_params=_cparams(2),
        name="gates_and_attention",
    )(sink, h, w_mg, q_g, k_g, q, ag, kv, kv, kv, kv, kv, kv, kv_ctx, kv_ctx, rope_tab, rope_tab, rope_tab)


def _ctx_attn_kernel(sink_ref, q_ref, kv_ref, ag_ref, qg_ref, kg_ref, o_ref):
    qg = qg_ref[...]
    kg = kg_ref[...]
    for h in range(N_KV_HEADS):
        kx = _rms_head(kv_ref[:, h * HEAD_DIM:(h + 1) * HEAD_DIM].astype(F32), kg).astype(BF16)
        v = kv_ref[:, KV_W + h * HEAD_DIM:KV_W + (h + 1) * HEAD_DIM]
        for j in range(GROUP):
            hh = h * GROUP + j
            cs = slice(hh * HEAD_DIM, (hh + 1) * HEAD_DIM)
            qn = (_rms_head(q_ref[:, cs].astype(F32), qg) * ATTN_SCALE).astype(BF16)
            s = _dot_nt(qn, kx)
            sink = sink_ref[hh]
            m = jnp.maximum(jnp.max(s, axis=-1, keepdims=True), sink)
            p = jnp.exp(s - m)
            den = jnp.sum(p, axis=-1, keepdims=True) + jnp.exp(sink - m)
            o = _dot(p.astype(BF16), v) * (1.0 / den)
            o_ref[h, :, j * HEAD_DIM:(j + 1) * HEAD_DIM] = (o * ag_ref[:, cs].astype(F32)).astype(o_ref.dtype)


def _context_attention(q, kv, ag, sink, q_g, k_g):
    return pl.pallas_call(
        _ctx_attn_kernel,
        grid=(BATCH,),
        in_specs=[pl.BlockSpec(memory_space=pltpu.SMEM),
                  pl.BlockSpec((CTX_LEN, ATTN_W), lambda b: (b, 0)),
                  pl.BlockSpec((CTX_LEN, 2 * KV_W), lambda b: (b, 0)),
                  pl.BlockSpec((CTX_LEN, ATTN_W), lambda b: (b, 0)),
                  pl.BlockSpec((1, HEAD_DIM), lambda b: (0, 0)),
                  pl.BlockSpec((1, HEAD_DIM), lambda b: (0, 0))],
        out_specs=pl.BlockSpec((N_KV_HEADS, CTX_LEN, GROUP_W), lambda b: (0, b, 0)),
        out_shape=jax.ShapeDtypeStruct((N_KV_HEADS, BATCH * CTX_LEN, GROUP_W), BF16),
        compiler_params=_cparams(1),
        name="context_attention",
    )(sink, q, kv, ag, q_g, k_g)


def _table(values):
    return jnp.asarray(values, F32).astype(BF16)


def _channel_dft_tables():
    n = FOURIER_GROUP_W
    ang = 2.0 * np.pi * np.outer(np.arange(n), np.arange(n)) / n
    s = n ** -0.5
    return _table(np.cos(ang) * s), _table(np.sin(ang) * s)


def _stage1_table():
    ang = 2.0 * np.pi * np.outer(np.arange(FFT_N2), np.arange(FFT_N2)) / FFT_N2
    s = FFT_N2 ** -0.5
    return _table(np.concatenate([np.cos(ang), -np.sin(ang)], axis=0) * s)


def _stage2_table():
    k2 = np.arange(FFT_N2)[:, None, None]
    k1 = np.arange(FFT_N1)[None, :, None]
    l1 = np.arange(FFT_N1)[None, None, :]
    ang = 2.0 * np.pi * ((l1 * (k2 + FFT_N2 * k1)) % SEQ) / SEQ
    tr = np.cos(ang) * FFT_N1 ** -0.5
    ti = -np.sin(ang) * FFT_N1 ** -0.5
    top = np.concatenate([tr, -ti], axis=2)
    bot = np.concatenate([ti, tr], axis=2)
    return _table(np.concatenate([top, bot], axis=1))


def _interleave_table():
    n = 8 * FFT_KB
    p = np.zeros((n, n), np.float32)
    for b in range(8):
        for j in range(FFT_KB):
            p[b * FFT_KB + j, j * 8 + b] = 1.0
    return _table(p)


def _fft1_kernel(d_ref, u_ref, o_ref):
    o_ref[...] = _dot(d_ref[...], u_ref[...]).astype(o_ref.dtype)


def _fft2_kernel(z_ref, t_ref, cc_ref, sc_ref, w_ref, p_ref, fg_ref, o_ref, ybuf, rbuf, ibuf):
    for j in range(FFT_KB):
        zst = jnp.concatenate([z_ref[0, j], z_ref[1, j]], axis=0)
        y = _dot(t_ref[j], zst)
        rbuf[j * FFT_N1:(j + 1) * FFT_N1, :] = y[:FFT_N1].astype(BF16)
        ibuf[j * FFT_N1:(j + 1) * FFT_N1, :] = y[FFT_N1:].astype(BF16)
    a = _dot(rbuf[...], cc_ref[...]) + _dot(ibuf[...], sc_ref[...])
    ybuf[...] = _dot(a.astype(BF16), w_ref[...])
    perm = p_ref[...]
    for a8 in range(FFT_N1 // 8):
        blk = jnp.concatenate([ybuf[j * FFT_N1 + 8 * a8:j * FFT_N1 + 8 * a8 + 8, :] for j in range(FFT_KB)], axis=0)
        nat = _dot(perm, blk.astype(BF16))
        gate = fg_ref[8 * a8:8 * a8 + 8].reshape(8 * FFT_KB, FOURIER_GROUP_W).astype(F32)
        o_ref[8 * a8:8 * a8 + 8] = (nat * gate).astype(o_ref.dtype).reshape(8, FFT_KB, FOURIER_GROUP_W)


def _latent_fourier(u, fg, w_mix, layer, tabs):
    d1, t2, cc, sc, perm = tabs
    cols = FFT_N1 * FOURIER_W
    tn = 16384
    z = pl.pallas_call(
        _fft1_kernel,
        grid=(BATCH, cols // tn),
        in_specs=[pl.BlockSpec((2 * FFT_N2, FFT_N2), lambda b, j: (0, 0)),
                  pl.BlockSpec((None, FFT_N2, tn), lambda b, j: (b, 0, j))],
        out_specs=pl.BlockSpec((None, 2 * FFT_N2, tn), lambda b, j: (b, 0, j)),
        out_shape=jax.ShapeDtypeStruct((BATCH, 2 * FFT_N2, cols), BF16),
        compiler_params=_cparams(2),
        name="fourier_stage1",
    )(d1, u.reshape(BATCH, FFT_N2, cols))
    z = z.reshape(BATCH, 2, FFT_N2, FFT_N1, FOURIER_W)
    gw = FOURIER_GROUP_W
    out = pl.pallas_call(
        _fft2_kernel,
        grid=(BATCH, FFT_N2 // FFT_KB, FOURIER_GROUPS),
        in_specs=[pl.BlockSpec((None, 2, FFT_KB, FFT_N1, gw), lambda b, t, g: (b, 0, t, 0, g)),
                  pl.BlockSpec((FFT_KB, 2 * FFT_N1, 2 * FFT_N1), lambda b, t, g: (t, 0, 0)),
                  pl.BlockSpec((gw, gw), lambda b, t, g: (0, 0)),
                  pl.BlockSpec((gw, gw), lambda b, t, g: (0, 0)),
                  pl.BlockSpec((None, None, gw, gw), lambda b, t, g: (layer, g, 0, 0)),
                  pl.BlockSpec((8 * FFT_KB, 8 * FFT_KB), lambda b, t, g: (0, 0)),
                  pl.BlockSpec((None, FFT_N1, FFT_KB, gw), lambda b, t, g: (b, 0, t, g))],
        out_specs=pl.BlockSpec((None, FFT_N1, FFT_KB, gw), lambda b, t, g: (b, 0, t, g)),
        out_shape=jax.ShapeDtypeStruct((BATCH, FFT_N1, FFT_N2, FOURIER_W), BF16),
        scratch_shapes=[pltpu.VMEM((FFT_KB * FFT_N1, gw), F32),
                        pltpu.VMEM((FFT_KB * FFT_N1, gw), BF16), pltpu.VMEM((FFT_KB * FFT_N1, gw), BF16)],
        compiler_params=_cparams(3),
        name="fourier_stage2",
    )(z, t2, cc, sc, w_mix, perm, fg.reshape(BATCH, FFT_N1, FFT_N2, FOURIER_W))
    return out.reshape(BATCH * SEQ, FOURIER_W)


def _ctx_dft_table():
    n = CTX_LEN
    ang = 2.0 * np.pi * np.outer(np.arange(n), np.arange(n)) / n
    s = n ** -0.5
    return _table(np.concatenate([np.cos(ang), -np.sin(ang)], axis=1) * s)


def _ctx_fourier_kernel(u_ref, fg_ref, cls_ref, cc_ref, sc_ref, w_ref, o_ref):
    u = u_ref[...]
    st = jnp.concatenate([_dot(u, cc_ref[...]), _dot(u, sc_ref[...])], axis=0).astype(BF16)
    f = _dot(cls_ref[...], st)
    y = _dot(f.astype(BF16), w_ref[...])
    o_ref[...] = (y * fg_ref[...].astype(F32)).astype(o_ref.dtype)


def _context_fourier(u, fg, w_mix, layer, cls, cc, sc):
    gw = FOURIER_GROUP_W
    return pl.pallas_call(
        _ctx_fourier_kernel,
        grid=(BATCH, FOURIER_GROUPS),
        in_specs=[pl.BlockSpec((CTX_LEN, gw), lambda b, g: (b, g)),
                  pl.BlockSpec((CTX_LEN, gw), lambda b, g: (b, g)),
                  pl.BlockSpec((CTX_LEN, 2 * CTX_LEN), lambda b, g: (0, 0)),
                  pl.BlockSpec((gw, gw), lambda b, g: (0, 0)),
                  pl.BlockSpec((gw, gw), lambda b, g: (0, 0)),
                  pl.BlockSpec((None, None, gw, gw), lambda b, g: (layer, g, 0, 0))],
        out_specs=pl.BlockSpec((CTX_LEN, gw), lambda b, g: (b, g)),
        out_shape=jax.ShapeDtypeStruct((BATCH * CTX_LEN, FOURIER_W), BF16),
        compiler_params=_cparams(2),
        name="context_fourier",
    )(u, fg, cls, cc, sc, w_mix)


def _conv_norm_act(u_ext, dww_ref, dwb_ref, lng_ref, lnb_ref, ubuf, tr):
    n_slab = CONV_W // LANES
    for c in range(n_slab):
        ubuf[c] = u_ext[:, c * LANES:(c + 1) * LANES]
    first = CONV_HALO - CONV_K // 2
    ys = []
    for c in range(n_slab):
        cs = slice(c * LANES, (c + 1) * LANES)
        acc = ubuf[c, first:first + tr, :] * dww_ref[0:1, cs]
        for k in range(1, CONV_K):
            acc = acc + ubuf[c, first + k:first + k + tr, :] * dww_ref[k:k + 1, cs]
        ys.append(acc)
    y = jnp.concatenate(ys, axis=1) + dwb_ref[...]
    mu = jnp.mean(y, axis=-1, keepdims=True)
    yc = y - mu
    var = jnp.mean(yc * yc, axis=-1, keepdims=True)
    return _silu(yc * lax.rsqrt(var + EPS) * lng_ref[...] + lnb_ref[...]).astype(BF16)


def _conv_kernel(a_ref, b_ref, ap_ref, bp_ref, an_ref, bn_ref, cg_ref, dww_ref, dwb_ref, lng_ref, lnb_ref,
                 wpw_ref, o_ref, ubuf, *, tiles_per_seq, tr):
    ti = lax.rem(pl.program_id(0), tiles_per_seq)
    u = a_ref[...].astype(F32) * b_ref[...].astype(F32)
    up = jnp.where(ti > 0, ap_ref[...].astype(F32) * bp_ref[...].astype(F32), 0.0)
    un = jnp.where(ti < tiles_per_seq - 1, an_ref[...].astype(F32) * bn_ref[...].astype(F32), 0.0)
    z = _conv_norm_act(jnp.concatenate([up, u, un], axis=0), dww_ref, dwb_ref, lng_ref, lnb_ref, ubuf, tr)
    o_ref[...] = (_dot(z, wpw_ref[...]) * cg_ref[...].astype(F32)).astype(o_ref.dtype)


def _conv_specs(m, tr, layer, tile_of):
    hb = tr // CONV_HALO
    n_hb = m // CONV_HALO
    tile = pl.BlockSpec((tr, CONV_W), lambda *idx: (tile_of(*idx), 0))
    halo_p = pl.BlockSpec((CONV_HALO, CONV_W), lambda *idx: (jnp.maximum(tile_of(*idx) * hb - 1, 0), 0))
    halo_n = pl.BlockSpec((CONV_HALO, CONV_W), lambda *idx: (jnp.minimum(tile_of(*idx) * hb + hb, n_hb - 1), 0))
    vec = pl.BlockSpec((None, 1, CONV_W), lambda *idx: (layer, 0, 0))
    specs = [tile, tile, halo_p, halo_p, halo_n, halo_n, tile,
             pl.BlockSpec((None, CONV_K, CONV_W), lambda *idx: (layer, 0, 0)),
             vec, vec, vec,
             pl.BlockSpec((None, CONV_W, CONV_W), lambda *idx: (layer, 0, 0))]
    return specs, tile


def _conv_args(a, b_sig, cg, dw_w, dw_b, ln_g, ln_b, w_pw):
    return (a, b_sig, a, b_sig, a, b_sig, cg, dw_w,
            dw_b.reshape(DEPTH, 1, CONV_W), ln_g.reshape(DEPTH, 1, CONV_W), ln_b.reshape(DEPTH, 1, CONV_W), w_pw)


def _conformer(a, b_sig, cg, dw_w, dw_b, ln_g, ln_b, w_pw, layer, seq_len, tr):
    m = a.shape[0]
    specs, tile = _conv_specs(m, tr, layer, lambda i: i)
    return pl.pallas_call(
        functools.partial(_conv_kernel, tiles_per_seq=seq_len // tr, tr=tr),
        grid=(m // tr,),
        in_specs=specs,
        out_specs=tile,
        out_shape=jax.ShapeDtypeStruct((m, CONV_W), BF16),
        scratch_shapes=[pltpu.VMEM((CONV_W // LANES, tr + 2 * CONV_HALO, LANES), F32)],
        compiler_params=_cparams(1),
        name="conformer_conv",
    )(*_conv_args(a, b_sig, cg, dw_w, dw_b, ln_g, ln_b, w_pw))


def _merge_kernel(a_ref, f_ref, c_ref, g0_ref, g1_ref, g2_ref, wa_ref, wf_ref, wc_ref, o_ref):
    y_attn = _dot(a_ref[0], wa_ref[0:GROUP_W, :])
    for g in range(1, N_KV_HEADS):
        y_attn = y_attn + _dot(a_ref[g], wa_ref[g * GROUP_W:(g + 1) * GROUP_W, :])
    merged = (g0_ref[...].astype(F32) * y_attn
              + g1_ref[...].astype(F32) * _dot(f_ref[...], wf_ref[...])
              + g2_ref[...].astype(F32) * _dot(c_ref[...], wc_ref[...]))
    o_ref[...] = merged.astype(o_ref.dtype)


def _merge(attn, four, conv, mg, w_attn_up, w_four_up, w_conv_up, layer, tm):
    m = four.shape[0]
    tn = 1024
    nj = D_MODEL // tn
    return pl.pallas_call(
        _merge_kernel,
        grid=(nj, m // tm),
        in_specs=[pl.BlockSpec((N_KV_HEADS, tm, GROUP_W), lambda j, i: (0, i, 0)),
                  pl.BlockSpec((tm, FOURIER_W), lambda j, i: (i, 0)),
                  pl.BlockSpec((tm, CONV_W), lambda j, i: (i, 0)),
                  pl.BlockSpec((tm, tn), lambda j, i: (i, j)),
                  pl.BlockSpec((tm, tn), lambda j, i: (i, nj + j)),
                  pl.BlockSpec((tm, tn), lambda j, i: (i, 2 * nj + j)),
                  pl.BlockSpec((None, ATTN_W, tn), lambda j, i: (layer, 0, j)),
                  pl.BlockSpec((None, FOURIER_W, tn), lambda j, i: (layer, 0, j)),
                  pl.BlockSpec((None, CONV_W, tn), lambda j, i: (layer, 0, j))],
        out_specs=pl.BlockSpec((tm, tn), lambda j, i: (i, j)),
        out_shape=jax.ShapeDtypeStruct((m, D_MODEL), BF16),
        compiler_params=_cparams(2),
        name="branch_merge",
    )(attn, four, conv, mg, mg, mg, w_attn_up, w_four_up, w_conv_up)


def _out_kernel(a_ref, w_ref, x_ref, gt_ref, o_ref):
    for rs in _row_parts(a_ref.shape[0]):
        o_ref[rs, :] = x_ref[rs, :] + gt_ref[...] * _dot(a_ref[rs, :], w_ref[...])


def _out_proj(merged, w_out, x2d, mods, layer, row_of_tile, tm):
    m = merged.shape[0]
    tn = 1024
    return pl.pallas_call(
        _out_kernel,
        grid=(D_MODEL // tn, m // tm),
        in_specs=[pl.BlockSpec((tm, D_MODEL), lambda j, i: (i, 0)),
                  pl.BlockSpec((None, D_MODEL, tn), lambda j, i: (layer, 0, j)),
                  pl.BlockSpec((tm, tn), lambda j, i: (i, j)),
                  pl.BlockSpec((None, None, 1, tn), lambda j, i: (layer, row_of_tile(i), 0, 2 * (D_MODEL // tn) + j))],
        out_specs=pl.BlockSpec((tm, tn), lambda j, i: (i, j)),
        out_shape=jax.ShapeDtypeStruct((m, D_MODEL), F32),
        compiler_params=_cparams(2),
        name="out_proj",
    )(merged, w_out, x2d, mods)


def _branches_and_merge(h, w, layer, tm, gates_and_attention, fourier, conv_seq, conv_tr):
    w_in = w["w_in"]
    conv_w = (w["conv_dw_w"], w["conv_dw_b"], w["conv_ln_g"], w["conv_ln_b"], w["w_conv_pw"])
    ca = _proj(h, w_in, layer, CA_OFF, CONV_W, "none", tm)
    cb = _proj(h, w_in, layer, CB_OFF, CONV_W, "sigmoid", tm)
    cg = _proj(h, w_in, layer, CG_OFF, CONV_W, "silu", tm)
    conv = _conformer(ca, cb, cg, *conv_w, layer, conv_seq, conv_tr)
    q = _proj(h, w_in, layer, Q_OFF, ATTN_W, "none", tm)
    kv = _proj(h, w_in, layer, K_OFF, 2 * KV_W, "none", tm)
    ag = _proj(h, w_in, layer, AG_OFF, ATTN_W, "silu", tm)
    fu = _proj(h, w_in, layer, F_OFF, FOURIER_W, "none", tm)
    fg = _proj(h, w_in, layer, FG_OFF, FOURIER_W, "silu", tm)
    mg, attn = gates_and_attention(h, q, kv, ag)
    four = fourier(fu, fg)
    merged = _merge(attn, four, conv, mg, w["w_attn_up"], w["w_fourier_up"], w["w_conv_up"], layer, min(tm, 512))
    return merged, kv


@jax.jit
def _block(x, c, ctx, c_ctx, norm_g, w_mod, b_mod, w_in, q_norm_g, k_norm_g, attn_sink, w_attn_up,
           w_fourier_mix, w_fourier_up, conv_dw_w, conv_dw_b, conv_ln_g, conv_ln_b, w_conv_pw, w_conv_up, w_out):
    w = dict(w_in=_cast_cols(w_in, 0, MG_OFF), w_mg=_cast_cols(w_in, MG_OFF, N_BRANCH * D_MODEL),
             w_attn_up=w_attn_up.astype(BF16), w_fourier_up=w_fourier_up.astype(BF16),
             w_conv_pw=w_conv_pw.astype(BF16), w_conv_up=w_conv_up.astype(BF16),
             conv_dw_w=conv_dw_w, conv_dw_b=conv_dw_b, conv_ln_g=conv_ln_g, conv_ln_b=conv_ln_b)
    w_out_b = w_out.astype(BF16)
    w_mix_b = w_fourier_mix.astype(BF16)

    rope_tab = _rope_tables()
    cc_t, sc_t = _channel_dft_tables()
    lat_tabs = (_stage1_table(), _stage2_table(), cc_t, sc_t, _interleave_table())
    cls_t = _ctx_dft_table()

    cond = jnp.concatenate([c, c_ctx[None, :], jnp.zeros((8 - BATCH - 1, D_MODEL), F32)], axis=0)
    mods = _modulation(cond, w_mod, b_mod).reshape(DEPTH, 8, 1, 3 * D_MODEL)
    ctx_row = BATCH

    x2d = x.reshape(BATCH * SEQ, D_MODEL)
    ctx2d = ctx.reshape(BATCH * CTX_LEN, D_MODEL)
    lat_tm = 1024
    ctx_tm = BATCH * CTX_LEN
    norm_tm = 512
    lat_tiles_per_batch = SEQ // norm_tm

    for l in range(DEPTH):
        sink = attn_sink[l]
        qg = q_norm_g[l].reshape(1, HEAD_DIM)
        kg = k_norm_g[l].reshape(1, HEAD_DIM)

        h_ctx = _adaln(ctx2d, norm_g, mods, l, lambda i: ctx_row, norm_tm)
        if l < DEPTH - 1:
            merged_c, kv_c = _branches_and_merge(
                h_ctx, w, l, ctx_tm,
                gates_and_attention=lambda h_, q, kv, ag: (
                    _proj(h_, w["w_mg"], l, 0, N_BRANCH * D_MODEL, "sigmoid", ctx_tm),
                    _context_attention(q, kv, ag, sink, qg, kg)),
                fourier=lambda fu, fg: _context_fourier(fu, fg, w_mix_b, l, cls_t, cc_t, sc_t),
                conv_seq=CTX_LEN, conv_tr=CTX_LEN)
            ctx_next = _out_proj(merged_c, w_out_b, ctx2d, mods, l, lambda i: ctx_row, ctx_tm)
        else:
            kv_c = _proj(h_ctx, w["w_in"], l, K_OFF, 2 * KV_W, "none", ctx_tm)
            ctx_next = ctx2d

        h = _adaln(x2d, norm_g, mods, l, lambda i: i // lat_tiles_per_batch, norm_tm)
        merged, _ = _branches_and_merge(
            h, w, l, lat_tm,
            gates_and_attention=lambda h_, q, kv, ag: _gates_and_attention(
                h_, w["w_mg"], l, q, kv, ag, kv_c, sink, qg, kg, rope_tab),
            fourier=lambda fu, fg: _latent_fourier(fu, fg, w_mix_b, l, lat_tabs),
            conv_seq=SEQ, conv_tr=256)
        out_tm = 1024
        x2d = _out_proj(merged, w_out_b, x2d, mods, l, lambda i: i // (SEQ // out_tm), out_tm)
        ctx2d = ctx_next
    return x2d.reshape(BATCH, SEQ, D_MODEL)


def kernel(x, c, ctx, c_ctx, norm_g, w_mod, b_mod, w_in, q_norm_g, k_norm_g, attn_sink, w_attn_up, w_fourier_mix,
           w_fourier_up, conv_dw_w, conv_dw_b, conv_ln_g, conv_ln_b, w_conv_pw, w_conv_up, w_out):
    return _block(x, c, ctx, c_ctx, norm_g, w_mod, b_mod, w_in, q_norm_g, k_norm_g, attn_sink, w_attn_up,
                  w_fourier_mix, w_fourier_up, conv_dw_w, conv_dw_b, conv_ln_g, conv_ln_b, w_conv_pw, w_conv_up,
                  w_out)
```
